```python
import jax, jax.numpy as jnp
from jax import lax
import numpy as np

D_MODEL = 1024
BATCH = 2
SEQ = 16384
DEPTH = 4

CONV_WIDTH = 512
CONV_K = 3
SWA_HEADS = 8
SWA_KV_HEADS = 2
SWA_HEAD_DIM = 64
WINDOW = 128
SWA_BLOCK = 128
GLA_HEADS = 4
GLA_DK = 64
GLA_DV = 128
GLA_GATE_RANK = 16
GLA_GATE_TAU = 16.0
GLA_CHUNK = 64
MLA_HEADS = 4
MLA_Q_RANK = 256
MLA_KV_RANK = 128
MLA_NOPE = 64
MLA_ROPE = 32
MLA_V = 128
MLA_BLOCK = 128
ROPE_THETA = 10000.0
D_FF = ((-(-8 * D_MODEL // 3) + 255) // 256) * 256
ALPHA = (2.0 * DEPTH) ** 0.25
BETA = (8.0 * DEPTH) ** -0.25
LN_EPS = 1e-5
RMS_EPS = 1e-6

N_EVEN = (DEPTH + 1) // 2
N_ODD = DEPTH // 2

EVEN_SPLITS = [CONV_WIDTH, CONV_WIDTH, CONV_WIDTH,
               SWA_HEADS * SWA_HEAD_DIM, SWA_KV_HEADS * SWA_HEAD_DIM, SWA_KV_HEADS * SWA_HEAD_DIM]
EVEN_IN = sum(EVEN_SPLITS)
EVEN_MIX = CONV_WIDTH + SWA_HEADS * SWA_HEAD_DIM
ODD_SPLITS = [GLA_HEADS * GLA_DK, GLA_HEADS * GLA_DK, GLA_HEADS * GLA_DV, GLA_GATE_RANK,
              GLA_HEADS * GLA_DV, MLA_Q_RANK, MLA_KV_RANK, MLA_ROPE]
ODD_IN = sum(ODD_SPLITS)
ODD_MIX = GLA_HEADS * GLA_DV + MLA_HEADS * MLA_V

kernel_name = "hybrid_conv_swa_gla_mla_deepnorm"


def split_cols(u, sizes):
    return jnp.split(u, np.cumsum(sizes)[:-1].tolist(), axis=-1)


def layer_norm(x, g, b):
    xf = x.astype(jnp.float32)
    mu = xf.mean(-1, keepdims=True)
    var = jnp.mean(jnp.square(xf - mu), -1, keepdims=True)
    return ((xf - mu) * lax.rsqrt(var + LN_EPS) * g.astype(jnp.float32) + b.astype(jnp.float32)).astype(x.dtype)


def rms_norm(x, g):
    xf = x.astype(jnp.float32)
    ms = jnp.mean(jnp.square(xf), -1, keepdims=True)
    return (xf * lax.rsqrt(ms + RMS_EPS) * g.astype(jnp.float32)).astype(x.dtype)


def apply_rope(t, cos, sin):
    t1, t2 = jnp.split(t, 2, axis=-1)
    return jnp.concatenate([t1 * cos - t2 * sin, t2 * cos + t1 * sin], axis=-1)


def short_conv_mixer(b_gate, c_gate, h, conv_w):
    S = h.shape[1]
    z = c_gate * h
    zp = jnp.pad(z, ((0, 0), (CONV_K - 1, 0), (0, 0)))
    y = zp[:, 0:S] * conv_w[0]
    for j in range(1, CONV_K):
        y = y + zp[:, j:j + S] * conv_w[j]
    return b_gate * y


def swa_sink_attention(q, k, v, sinks):
    B_, S, _ = q.shape
    nb = S // SWA_BLOCK
    G = SWA_HEADS // SWA_KV_HEADS
    q = q.reshape(B_, nb, SWA_BLOCK, SWA_KV_HEADS, G, SWA_HEAD_DIM)
    k = k.reshape(B_, nb, SWA_BLOCK, SWA_KV_HEADS, SWA_HEAD_DIM)
    v = v.reshape(B_, nb, SWA_BLOCK, SWA_KV_HEADS, SWA_HEAD_DIM)
    pad = ((0, 0), (1, 0), (0, 0), (0, 0), (0, 0))
    kk = jnp.concatenate([jnp.pad(k, pad)[:, :-1], k], axis=2)
    vv = jnp.concatenate([jnp.pad(v, pad)[:, :-1], v], axis=2)
    s = jnp.einsum('bnqhgd,bnkhd->bnhgqk', q, kk).astype(jnp.float32) * (SWA_HEAD_DIM ** -0.5)
    qi = jnp.arange(SWA_BLOCK)[:, None]
    kj = jnp.arange(2 * SWA_BLOCK)[None, :] - SWA_BLOCK
    diff = qi - kj
    band = (diff >= 0) & (diff < WINDOW)
    valid = (jnp.arange(nb)[:, None, None] > 0) | (kj[None] >= 0)
    mask = band[None] & valid
    s = jnp.where(mask[None, :, None, None], s, -jnp.inf)
    sink = sinks.astype(jnp.float32).reshape(SWA_KV_HEADS, G)[None, None, :, :, None, None]
    m = jnp.maximum(s.max(-1, keepdims=True), sink)
    p = jnp.exp(s - m)
    p = p / (p.sum(-1, keepdims=True) + jnp.exp(sink - m))
    o = jnp.einsum('bnhgqk,bnkhd->bnqhgd', p.astype(v.dtype), vv)
    return o.reshape(B_, S, SWA_HEADS * SWA_HEAD_DIM)


def gla_mixer(q, k, v, g_low, r, w_gate, b_gate, g_norm):
    B_, S, _ = q.shape
    H, C = GLA_HEADS, GLA_CHUNK
    nc = S // C
    log_a = jax.nn.log_sigmoid((g_low @ w_gate + b_gate).astype(jnp.float32)) / GLA_GATE_TAU

    def to_chunks(t, d):
        return t.astype(jnp.float32).reshape(B_, nc, C, H, d).transpose(1, 0, 3, 2, 4)

    qc = to_chunks(q, GLA_DK) * (GLA_DK ** -0.5)
    kc = to_chunks(k, GLA_DK)
    vc = to_chunks(v, GLA_DV)
    gc = to_chunks(log_a, GLA_DK)
    causal = jnp.tril(jnp.ones((C, C), bool))[:, :, None]

    def step(state, inp):
        qb, kb, vb, gb = inp
        b = jnp.cumsum(gb, axis=2)
        o_inter = jnp.einsum('bhtd,bhdv->bhtv', qb * jnp.exp(b), state)
        diff = b[:, :, :, None, :] - b[:, :, None, :, :]
        decay = jnp.exp(jnp.where(causal, diff, -jnp.inf))
        attn = jnp.einsum('bhtd,bhsd,bhtsd->bhts', qb, kb, decay)
        o = o_inter + jnp.einsum('bhts,bhsv->bhtv', attn, vb)
        b_last = b[:, :, -1:, :]
        k_dec = kb * jnp.exp(b_last - b)
        state = jnp.exp(b_last[:, :, 0, :])[..., None] * state + jnp.einsum('bhsd,bhsv->bhdv', k_dec, vb)
        return state, o

    s0 = jnp.zeros((B_, H, GLA_DK, GLA_DV), jnp.float32)
    _, o = lax.scan(step, s0, (qc, kc, vc, gc))
    o = o.transpose(1, 0, 3, 2, 4).reshape(B_, S, H, GLA_DV)
    o = rms_norm(o, g_norm).reshape(B_, S, H * GLA_DV)
    return (o * jax.nn.silu(r.astype(jnp.float32))).astype(q.dtype)


def mla_mixer(c_q, c_kv, k_r, g_qn, w_uq, g_kvn, w_ukv, cos, sin):
    B_, S, _ = c_q.shape
    H = MLA_HEADS
    q = (rms_norm(c_q, g_qn) @ w_uq).reshape(B_, S, H, MLA_NOPE + MLA_ROPE)
    q_nope = q[..., :MLA_NOPE]
    q_rope = apply_rope(q[..., MLA_NOPE:], cos[:, None, :], sin[:, None, :])
    kv = (rms_norm(c_kv, g_kvn) @ w_ukv).reshape(B_, S, H, MLA_NOPE + MLA_V)
    k_nope, v = kv[..., :MLA_NOPE], kv[..., MLA_NOPE:]
    k_rope = apply_rope(k_r, cos, sin)
    scale = (MLA_NOPE + MLA_ROPE) ** -0.5
    nb = S // MLA_BLOCK
    qn_b = q_nope.reshape(B_, nb, MLA_BLOCK, H, MLA_NOPE).transpose(1, 0, 2, 3, 4)
    qr_b = q_rope.reshape(B_, nb, MLA_BLOCK, H, MLA_ROPE).transpose(1, 0, 2, 3, 4)
    kpos = jnp.arange(S)

    def attend(args):
        qn, qr, i = args
        s = (jnp.einsum('bqhd,bkhd->bhqk', qn, k_nope) +
             jnp.einsum('bqhr,bkr->bhqk', qr, k_rope)).astype(jnp.float32) * scale
        qpos = i * MLA_BLOCK + jnp.arange(MLA_BLOCK)
        s = jnp.where(kpos[None, :] <= qpos[:, None], s, -jnp.inf)
        p = jax.nn.softmax(s, axis=-1)
        return jnp.einsum('bhqk,bkhv->bqhv', p.astype(v.dtype), v)

    o = lax.map(attend, (qn_b, qr_b, jnp.arange(nb)))
    return o.transpose(1, 0, 2, 3, 4).reshape(B_, S, H * MLA_V)


def swiglu(x, w_gate, w_up, w_down):
    return (jax.nn.silu(x @ w_gate) * (x @ w_up)) @ w_down


def setup_inputs(seed: int = 0) -> dict:
    key = jax.random.key(seed)
    ks = iter(jax.random.split(key, 32))
    nrm = lambda shape, s: jax.random.normal(next(ks), shape, jnp.float32) * s
    D = D_MODEL
    return {
        "x": nrm((BATCH, SEQ, D), 1.0),
        "ev_w_in": nrm((N_EVEN, D, EVEN_IN), D ** -0.5),
        "ev_conv_w": nrm((N_EVEN, CONV_K, CONV_WIDTH), CONV_K ** -0.5),
        "ev_sinks": nrm((N_EVEN, SWA_HEADS), 0.5),
        "ev_w_out": nrm((N_EVEN, EVEN_MIX, D), BETA * EVEN_MIX ** -0.5),
        "od_w_in": nrm((N_ODD, D, ODD_IN), D ** -0.5),
        "od_gla_w_gate": nrm((N_ODD, GLA_GATE_RANK, GLA_HEADS * GLA_DK), GLA_GATE_RANK ** -0.5),
        "od_gla_b_gate": nrm((N_ODD, GLA_HEADS * GLA_DK), 0.1),
        "od_gla_norm_g": 1.0 + nrm((N_ODD, GLA_DV), 0.02),
        "od_mla_q_norm_g": 1.0 + nrm((N_ODD, MLA_Q_RANK), 0.02),
        "od_mla_w_uq": nrm((N_ODD, MLA_Q_RANK, MLA_HEADS * (MLA_NOPE + MLA_ROPE)), MLA_Q_RANK ** -0.5),
        "od_mla_kv_norm_g": 1.0 + nrm((N_ODD, MLA_KV_RANK), 0.02),
        "od_mla_w_ukv": nrm((N_ODD, MLA_KV_RANK, MLA_HEADS * (MLA_NOPE + MLA_V)), MLA_KV_RANK ** -0.5),
        "od_w_out": nrm((N_ODD, ODD_MIX, D), BETA * ODD_MIX ** -0.5),
        "ffn_w_gate": nrm((DEPTH, D, D_FF), D ** -0.5),
        "ffn_w_up": nrm((DEPTH, D, D_FF), D ** -0.5),
        "ffn_w_down": nrm((DEPTH, D_FF, D), BETA * D_FF ** -0.5),
        "ln_mix_g": 1.0 + nrm((DEPTH, D), 0.02),
        "ln_mix_b": nrm((DEPTH, D), 0.02),
        "ln_ffn_g": 1.0 + nrm((DEPTH, D), 0.02),
        "ln_ffn_b": nrm((DEPTH, D), 0.02),
    }


def reference(x, ev_w_in, ev_conv_w, ev_sinks, ev_w_out, od_w_in, od_gla_w_gate, od_gla_b_gate,
              od_gla_norm_g, od_mla_q_norm_g, od_mla_w_uq, od_mla_kv_norm_g, od_mla_w_ukv, od_w_out,
              ffn_w_gate, ffn_w_up, ffn_w_down, ln_mix_g, ln_mix_b, ln_ffn_g, ln_ffn_b):
    S = x.shape[1]
    pos = jnp.arange(S, dtype=jnp.float32)
    inv_freq = ROPE_THETA ** (-jnp.arange(0, MLA_ROPE, 2, dtype=jnp.float32) / MLA_ROPE)
    ang = pos[:, None] * inv_freq[None, :]
    cos, sin = jnp.cos(ang).astype(x.dtype), jnp.sin(ang).astype(x.dtype)

    for layer in range(DEPTH):
        i = layer // 2
        if layer % 2 == 0:
            u = x @ ev_w_in[i]
            b_g, c_g, h, q, k, v = split_cols(u, EVEN_SPLITS)
            ya = short_conv_mixer(b_g, c_g, h, ev_conv_w[i])
            yb = swa_sink_attention(q, k, v, ev_sinks[i])
            y = jnp.concatenate([ya, yb], axis=-1) @ ev_w_out[i]
        else:
            u = x @ od_w_in[i]
            gq, gk, gv, g_low, gr, c_q, c_kv, k_r = split_cols(u, ODD_SPLITS)
            yc = gla_mixer(gq, gk, gv, g_low, gr, od_gla_w_gate[i], od_gla_b_gate[i], od_gla_norm_g[i])
            yd = mla_mixer(c_q, c_kv, k_r, od_mla_q_norm_g[i], od_mla_w_uq[i],
                           od_mla_kv_norm_g[i], od_mla_w_ukv[i], cos, sin)
            y = jnp.concatenate([yc, yd], axis=-1) @ od_w_out[i]
        x = layer_norm(ALPHA * x + y, ln_mix_g[layer], ln_mix_b[layer])
        x = layer_norm(ALPHA * x + swiglu(x, ffn_w_gate[layer], ffn_w_up[layer], ffn_w_down[layer]),
                       ln_ffn_g[layer], ln_ffn_b[layer])
    return x
```

```python
import functools

import numpy as np
import jax
import jax.numpy as jnp
from jax import lax
from jax.experimental import pallas as pl
from jax.experimental.pallas import tpu as pltpu

F32 = jnp.float32
BF16 = jnp.bfloat16

D_MODEL = 1024
BATCH = 2
SEQ = 16384
DEPTH = 4
TOKENS = BATCH * SEQ

CONV_WIDTH = 512
CONV_K = 3
SWA_HEADS = 8
SWA_KV_HEADS = 2
SWA_HEAD_DIM = 64
SWA_BLOCK = 128
GLA_HEADS = 4
GLA_DK = 64
GLA_DV = 128
GLA_GATE_RANK = 16
GLA_GATE_TAU = 16.0
GLA_CHUNK = 64
MLA_HEADS = 4
MLA_Q_RANK = 256
MLA_KV_RANK = 128
MLA_NOPE = 64
MLA_ROPE = 32
MLA_V = 128
ROPE_THETA = 10000.0
D_FF = 2816
ALPHA = (2.0 * DEPTH) ** 0.25
LN_EPS = 1e-5
RMS_EPS = 1e-6

LANES = 128
SUBLANES = 8
VMEM_LIMIT = 56 * 1024 * 1024

EVEN_IN = 3 * CONV_WIDTH + SWA_HEADS * SWA_HEAD_DIM + 2 * SWA_KV_HEADS * SWA_HEAD_DIM
GLA_QK = GLA_HEADS * GLA_DK
GLA_VW = GLA_HEADS * GLA_DV
MLA_W = MLA_HEADS * LANES
ODD_EXT = 2 * GLA_QK + 2 * GLA_VW + MLA_Q_RANK + MLA_KV_RANK + 3 * LANES
GLA_LEVELS = 6
GLA_PROWS = (2 + GLA_LEVELS) * GLA_CHUNK

TM_EVEN = 512
TM_ODD = 512
TM_POST = 256
TG_GLA = 512
TQ_MLA = 512
TK_MLA = 512


def _params(*sem):
    return pltpu.CompilerParams(dimension_semantics=sem, vmem_limit_bytes=VMEM_LIMIT)


def _const_spec(shape):
    return pl.BlockSpec(shape, lambda *_: (0,) * len(shape))


def _layer_norm(v, g, b):
    mu = jnp.mean(v, axis=-1, keepdims=True)
    d = v - mu
    var = jnp.mean(d * d, axis=-1, keepdims=True)
    return d * lax.rsqrt(var + LN_EPS) * g + b


def _dot(a, b):
    return jnp.dot(a, b, preferred_element_type=F32)


def _dot_nt(a, b):
    return lax.dot_general(a, b, (((1,), (1,)), ((), ())), preferred_element_type=F32)


def _dot_tn(a, b):
    return lax.dot_general(a, b, (((0,), (0,)), ((), ())), preferred_element_type=F32)


def _post_kernel(x_ref, y1_ref, y2_ref, wo1_ref, wo2_ref, g1_ref, b1_ref,
                 wg_ref, wu_ref, wd_ref, g2_ref, b2_ref, o_ref):
    y = _dot(y1_ref[...], wo1_ref[...]) + _dot(y2_ref[...], wo2_ref[...])
    x1 = _layer_norm(ALPHA * x_ref[...] + y, g1_ref[...], b1_ref[...])
    xb = x1.astype(BF16)
    gate = _dot(xb, wg_ref[...])
    up = _dot(xb, wu_ref[...])
    h = (gate * jax.nn.sigmoid(gate) * up).astype(BF16)
    f = _dot(h, wd_ref[...])
    o_ref[...] = _layer_norm(ALPHA * x1 + f, g2_ref[...], b2_ref[...])


def _post_call(x, y1, y2, wo1, wo2, g1, b1, wg, wu, wd, g2, b2):
    tm = TM_POST
    half = D_MODEL // 2
    row = lambda w: pl.BlockSpec((tm, w), lambda i: (i, 0))
    return pl.pallas_call(
        _post_kernel,
        grid=(TOKENS // tm,),
        in_specs=[row(D_MODEL), row(half), row(half),
                  _const_spec((half, D_MODEL)), _const_spec((half, D_MODEL)),
                  _const_spec((1, D_MODEL)), _const_spec((1, D_MODEL)),
                  _const_spec((D_MODEL, D_FF)), _const_spec((D_MODEL, D_FF)),
                  _const_spec((D_FF, D_MODEL)),
                  _const_spec((1, D_MODEL)), _const_spec((1, D_MODEL))],
        out_specs=row(D_MODEL),
        out_shape=jax.ShapeDtypeStruct((TOKENS, D_MODEL), F32),
        compiler_params=_params("parallel"),
        name="post_ffn",
    )(x, y1, y2, wo1, wo2, g1, b1, wg, wu, wd, g2, b2)


def _even_kernel(x_ref, win_ref, cw_ref, sink_ref, ya_ref, yb_ref, zc_ref, kvp_ref, *, tm):
    i = pl.program_id(1)

    @pl.when(i == 0)
    def _():
        zc_ref[...] = jnp.zeros_like(zc_ref)
        kvp_ref[...] = jnp.zeros_like(kvp_ref)

    u = _dot(x_ref[...].astype(BF16), win_ref[...])
    cw = CONV_WIDTH
    b_gate = u[:, 0:cw]
    z = u[:, cw:2 * cw] * u[:, 2 * cw:3 * cw]

    w0 = cw_ref[0:1, :]
    w1 = cw_ref[1:2, :]
    w2 = cw_ref[2:3, :]
    r1 = pltpu.roll(z, 1, 0)
    r2 = pltpu.roll(z, 2, 0)
    c8 = zc_ref[...]
    c1 = pltpu.roll(c8, 1, 0)
    c2 = pltpu.roll(c8, 2, 0)
    rid = lax.broadcasted_iota(jnp.int32, (SUBLANES, cw), 0)
    top1 = jnp.where(rid < 1, c1, r1[0:SUBLANES])
    top2 = jnp.where(rid < 2, c2, r2[0:SUBLANES])
    y_top = top2 * w0 + top1 * w1 + z[0:SUBLANES] * w2
    y_rest = r2[SUBLANES:] * w0 + r1[SUBLANES:] * w1 + z[SUBLANES:] * w2
    ya_ref[0:SUBLANES, :] = (b_gate[0:SUBLANES] * y_top).astype(BF16)
    ya_ref[SUBLANES:, :] = (b_gate[SUBLANES:] * y_rest).astype(BF16)
    zc_ref[...] = z[tm - SUBLANES:tm]

    qoff = 3 * cw
    koff = qoff + SWA_HEADS * SWA_HEAD_DIM
    voff = koff + SWA_KV_HEADS * SWA_HEAD_DIM
    q = u[:, qoff:koff] * (SWA_HEAD_DIM ** -0.5)
    k = u[:, koff:voff].astype(BF16)
    v = u[:, voff:voff + LANES].astype(BF16)
    blk = SWA_BLOCK
    r = lax.broadcasted_iota(jnp.int32, (blk, 2 * blk), 0)
    c = lax.broadcasted_iota(jnp.int32, (blk, 2 * blk), 1)
    own = (c >= blk) & ((c - blk) <= r)
    prev = (c < blk) & (c > r)
    prev_first = (c < blk) & (c > r + jnp.where(i > 0, 0, blk))
    lane = lax.broadcasted_iota(jnp.int32, (blk, LANES), 1)
    low_half = lane < SWA_HEAD_DIM
    group = SWA_HEADS // SWA_KV_HEADS
    for j in range(tm // blk):
        rows = slice(j * blk, (j + 1) * blk)
        if j == 0:
            k_prev = kvp_ref[:, 0:LANES]
            v_prev = kvp_ref[:, LANES:2 * LANES]
            mask = own | prev_first
        else:
            k_prev = k[(j - 1) * blk:j * blk]
            v_prev = v[(j - 1) * blk:j * blk]
            mask = own | prev
        kk = jnp.concatenate([k_prev, k[rows]], axis=0)
        vv = jnp.concatenate([v_prev, v[rows]], axis=0)
        for pair in range(SWA_HEADS // 2):
            qp = q[rows, pair * LANES:(pair + 1) * LANES]
            qp_sw = pltpu.roll(qp, SWA_HEAD_DIM, 1)
            g = (2 * pair) // group
            in_g = low_half if g == 0 else jnp.logical_not(low_half)
            outs = []
            for par in range(2):
                h = 2 * pair + par
                src = qp if par == g else qp_sw
                qh = jnp.where(in_g, src, 0.0).astype(BF16)
                s = _dot_nt(qh, kk)
                s = jnp.where(mask, s, -jnp.inf)
                sink = sink_ref[h]
                m = jnp.maximum(jnp.max(s, axis=-1, keepdims=True), sink)
                p = jnp.exp(s - m)
                denom = jnp.sum(p, axis=-1, keepdims=True) + jnp.exp(sink - m)
                o2 = _dot(p.astype(BF16), vv) / denom
                outs.append(o2)
            if g == 0:
                out = jnp.where(low_half, outs[0], pltpu.roll(outs[1], SWA_HEAD_DIM, 1))
            else:
                out = jnp.where(low_half, pltpu.roll(outs[0], SWA_HEAD_DIM, 1), outs[1])
            yb_ref[rows, pair * LANES:(pair + 1) * LANES] = out.astype(BF16)
    kvp_ref[:, 0:LANES] = k[tm - blk:tm]
    kvp_ref[:, LANES:2 * LANES] = v[tm - blk:tm]


def _even_call(x, w_in, conv_w, sinks):
    tm = TM_EVEN
    nt = SEQ // tm
    row = lambda w: pl.BlockSpec((tm, w), lambda b, i: (b * nt + i, 0))
    half = D_MODEL // 2
    return pl.pallas_call(
        functools.partial(_even_kernel, tm=tm),
        grid=(BATCH, nt),
        in_specs=[row(D_MODEL), _const_spec((D_MODEL, EVEN_IN)), _const_spec((CONV_K, CONV_WIDTH)),
                  pl.BlockSpec(memory_space=pltpu.SMEM)],
        out_specs=[row(half), row(half)],
        out_shape=[jax.ShapeDtypeStruct((TOKENS, half), BF16)] * 2,
        scratch_shapes=[pltpu.VMEM((SUBLANES, CONV_WIDTH), F32),
                        pltpu.VMEM((SWA_BLOCK, 2 * LANES), BF16)],
        compiler_params=_params("arbitrary", "arbitrary"),
        name="even_mixer",
    )(x, w_in, conv_w, sinks)


def _odd_proj_kernel(x_ref, win_ref, wgate_ref, bgate_ref, gq_ref, wq_ref, gkv_ref, wkv_ref,
                     ctab_ref, stab_ref,
                     gq_o, gk_o, gv_o, gr_o, la_o, q_o, k_o, v_o):
    u = _dot(x_ref[...].astype(BF16), win_ref[...])
    o = 0
    gq_o[...] = (u[:, o:o + GLA_QK] * (GLA_DK ** -0.5)).astype(BF16); o += GLA_QK
    gk_o[...] = u[:, o:o + GLA_QK].astype(BF16); o += GLA_QK
    gv_o[...] = u[:, o:o + GLA_VW].astype(BF16); o += GLA_VW
    gr_o[...] = u[:, o:o + GLA_VW].astype(BF16); o += GLA_VW
    c_q = u[:, o:o + MLA_Q_RANK]; o += MLA_Q_RANK
    c_kv = u[:, o:o + MLA_KV_RANK]; o += MLA_KV_RANK
    g_low = u[:, o:o + LANES]; o += LANES
    kr = u[:, o:o + LANES]; o += LANES
    kr_sw = u[:, o:o + LANES]

    xg = _dot(g_low.astype(BF16), wgate_ref[...]) + bgate_ref[...]
    log_sig = jnp.minimum(xg, 0.0) - jnp.log(1.0 + jnp.exp(-jnp.abs(xg)))
    la_o[...] = log_sig * (1.0 / GLA_GATE_TAU)

    ctab = ctab_ref[...]
    stab = stab_ref[...]
    cqn = c_q * lax.rsqrt(jnp.mean(c_q * c_q, axis=-1, keepdims=True) + RMS_EPS) * gq_ref[...]
    q2 = _dot(cqn.astype(BF16), wq_ref[...])
    scale = (MLA_NOPE + MLA_ROPE) ** -0.5
    ckvn = c_kv * lax.rsqrt(jnp.mean(c_kv * c_kv, axis=-1, keepdims=True) + RMS_EPS) * gkv_ref[...]
    kv = _dot(ckvn.astype(BF16), wkv_ref[...])
    k_rope = kr * ctab + kr_sw * stab
    for h in range(MLA_HEADS):
        cols = slice(h * LANES, (h + 1) * LANES)
        qa = q2[:, cols]
        qb = q2[:, MLA_W + h * LANES:MLA_W + (h + 1) * LANES]
        q_o[:, cols] = ((qa * ctab + qb * stab) * scale).astype(BF16)
        k_o[:, cols] = (kv[:, cols] + k_rope).astype(BF16)
    v_o[...] = kv[:, MLA_W:2 * MLA_W].astype(BF16)


def _odd_proj_call(x, w_in, wgate, bgate, gq, wq, gkv, wkv, ctab, stab):
    tm = TM_ODD
    nt = SEQ // tm
    row = lambda w: pl.BlockSpec((tm, w), lambda i: (i, 0))
    tab = pl.BlockSpec((tm, LANES), lambda i: (i % nt, 0))
    outs = [(GLA_QK, BF16), (GLA_QK, BF16), (GLA_VW, BF16), (GLA_VW, BF16), (GLA_QK, F32),
            (MLA_W, BF16), (MLA_W, BF16), (MLA_W, BF16)]
    return pl.pallas_call(
        _odd_proj_kernel,
        grid=(TOKENS // tm,),
        in_specs=[row(D_MODEL), _const_spec((D_MODEL, ODD_EXT)),
                  _const_spec((LANES, GLA_QK)), _const_spec((1, GLA_QK)),
                  _const_spec((1, MLA_Q_RANK)), _const_spec((MLA_Q_RANK, 2 * MLA_W)),
                  _const_spec((1, MLA_KV_RANK)), _const_spec((MLA_KV_RANK, 2 * MLA_W)),
                  tab, tab],
        out_specs=[row(w) for w, _ in outs],
        out_shape=[jax.ShapeDtypeStruct((TOKENS, w), dt) for w, dt in outs],
        compiler_params=_params("parallel"),
        name="odd_proj",
    )(x, w_in, wgate, bgate, gq, wq, gkv, wkv, ctab, stab)


def _gla_prefix_matrix():
    c = GLA_CHUNK
    t = np.arange(c)[:, None]
    u = np.arange(c)[None, :]
    blocks = [(u <= t), (u > t)]
    length = c // 2
    while length >= 1:
        rho = (t // (2 * length)) * 2 * length + length - 1
        right = (t % (2 * length)) >= length
        blocks.append(np.where(right, (u > rho) & (u <= t), (u > t) & (u <= rho)))
        length //= 2
    return np.concatenate(blocks, axis=0).astype(np.float32)


def _gla_pair_masks():
    c = GLA_CHUNK
    t = np.arange(c)[:, None]
    s = np.arange(c)[None, :]
    masks = []
    length = c // 2
    while length >= 1:
        same = (t // (2 * length)) == (s // (2 * length))
        masks.append(same & ((t % (2 * length)) >= length) & ((s % (2 * length)) < length))
        length //= 2
    masks.append(t == s)
    m = np.stack(masks).astype(np.float32)
    return np.tile(m, (1, 1, GLA_HEADS))


def _gla_kernel(q_ref, k_ref, v_ref, r_ref, g_ref, pm_ref, mk_ref, gn_ref, o_ref, st_ref, *, tg):
    @pl.when(pl.program_id(1) == 0)
    def _():
        st_ref[...] = jnp.zeros_like(st_ref)

    c = GLA_CHUNK
    nh = GLA_HEADS
    lane_qk = lax.broadcasted_iota(jnp.int32, (c, GLA_QK), 1)
    head_of_lane = [(lane_qk >= h * GLA_DK) & (lane_qk < (h + 1) * GLA_DK) for h in range(nh)]
    zeros_v = jnp.zeros((c, GLA_DV), BF16)
    ones_c = jnp.ones((c, LANES), BF16)
    gnorm = gn_ref[...]

    def stack_heads(a):
        return jnp.concatenate([jnp.where(head_of_lane[h], a, 0.0) for h in range(nh)],
                               axis=0).astype(BF16)

    def block_diag(blocks):
        rows = []
        for h in range(nh):
            rows.append(jnp.concatenate(
                [blocks[h] if hh == h else zeros_v for hh in range(nh)], axis=1))
        return jnp.concatenate(rows, axis=0)

    def chunk(ci, carry):
        r0 = pl.multiple_of(ci * c, c)
        rows = pl.ds(r0, c)
        q = q_ref[rows, :].astype(F32)
        k = k_ref[rows, :].astype(F32)
        v = v_ref[rows, :]
        g = g_ref[rows, :]
        g_hi = g.astype(BF16)
        g_lo = (g - g_hi.astype(F32)).astype(BF16)
        g2 = jnp.concatenate([g_hi, g_lo], axis=1)
        x2 = _dot(pm_ref[...], g2)
        xs = x2[:, 0:GLA_QK] + x2[:, GLA_QK:2 * GLA_QK]
        b = xs[0:c]
        qe = (q * jnp.exp(b)).astype(BF16)
        kd = (k * jnp.exp(xs[c:2 * c])).astype(BF16)

        att = _dot_nt(q.astype(BF16), stack_heads(k)) * mk_ref[GLA_LEVELS]
        for lv in range(GLA_LEVELS):
            e = jnp.exp(xs[(2 + lv) * c:(3 + lv) * c])
            att = att + _dot_nt((q * e).astype(BF16), stack_heads(k * e)) * mk_ref[lv]

        v_heads = [v[:, h * GLA_DV:(h + 1) * GLA_DV] for h in range(nh)]
        s_heads = [st_ref[h].astype(BF16) for h in range(nh)]
        lhs = jnp.concatenate([att.astype(BF16), qe], axis=1)
        rhs = jnp.concatenate([block_diag(v_heads), block_diag(s_heads)], axis=0)
        o = _dot(lhs, rhs)

        upd = _dot_tn(kd, v)
        bcol2 = _dot_tn(g2, ones_c)
        dcol = jnp.exp(bcol2[0:GLA_QK] + bcol2[GLA_QK:2 * GLA_QK])
        for h in range(nh):
            rs = slice(h * GLA_DK, (h + 1) * GLA_DK)
            st_ref[h] = dcol[rs] * st_ref[h] + upd[rs, h * GLA_DV:(h + 1) * GLA_DV]

        rr = r_ref[rows, :].astype(F32)
        gate = rr * jax.nn.sigmoid(rr)
        for h in range(nh):
            cs = slice(h * GLA_DV, (h + 1) * GLA_DV)
            oh = o[:, cs]
            ms = jnp.mean(oh * oh, axis=-1, keepdims=True)
            o_ref[rows, cs] = (oh * lax.rsqrt(ms + RMS_EPS) * gnorm * gate[:, cs]).astype(BF16)
        return carry

    lax.fori_loop(0, tg // c, chunk, 0)


def _gla_call(gq, gk, gv, gr, la, pm, mk, gn):
    tg = TG_GLA
    nt = SEQ // tg
    row = lambda w: pl.BlockSpec((tg, w), lambda b, i: (b * nt + i, 0))
    return pl.pallas_call(
        functools.partial(_gla_kernel, tg=tg),
        grid=(BATCH, nt),
        in_specs=[row(GLA_QK), row(GLA_QK), row(GLA_VW), row(GLA_VW), row(GLA_QK),
                  _const_spec((GLA_PROWS, GLA_CHUNK)),
                  _const_spec((GLA_LEVELS + 1, GLA_CHUNK, GLA_HEADS * GLA_CHUNK)),
                  _const_spec((1, GLA_DV))],
        out_specs=row(GLA_VW),
        out_shape=jax.ShapeDtypeStruct((TOKENS, GLA_VW), BF16),
        scratch_shapes=[pltpu.VMEM((GLA_HEADS, GLA_DK, GLA_DV), F32)],
        compiler_params=_params("arbitrary", "arbitrary"),
        name="gla_mixer",
    )(gq, gk, gv, gr, la, pm, mk, gn)


def _mla_kernel(q_ref, k_ref, v_ref, o_ref, m_ref, l_ref, acc_ref, *, tq, tk):
    qi = pl.program_id(2)
    q = q_ref[...]
    m_ref[...] = jnp.full_like(m_ref, -jnp.inf)
    l_ref[...] = jnp.zeros_like(l_ref)
    acc_ref[...] = jnp.zeros_like(acc_ref)

    def step(kj, masked):
        k0 = pl.multiple_of(kj * tk, tk)
        s = _dot_nt(q, k_ref[pl.ds(k0, tk), :])
        if masked:
            r = lax.broadcasted_iota(jnp.int32, (tq, tk), 0)
            c = lax.broadcasted_iota(jnp.int32, (tq, tk), 1)
            s = jnp.where(c <= r, s, -jnp.inf)
        m_prev = m_ref[...]
        m_new = jnp.maximum(m_prev, jnp.max(s, axis=-1, keepdims=True))
        a = jnp.exp(m_prev - m_new)
        p = jnp.exp(s - m_new)
        l_ref[...] = a * l_ref[...] + jnp.sum(p, axis=-1, keepdims=True)
        acc_ref[...] = a * acc_ref[...] + _dot(p.astype(BF16), v_ref[pl.ds(k0, tk), :])
        m_ref[...] = m_new

    def body(kj, carry):
        step(kj, False)
        return carry

    lax.fori_loop(0, qi, body, 0)
    step(qi, True)
    o_ref[...] = (acc_ref[...] / l_ref[...]).astype(BF16)


def _mla_call(q, k, v):
    tq, tk = TQ_MLA, TK_MLA
    assert tq == tk
    nq = SEQ // tq
    qspec = pl.BlockSpec((tq, LANES), lambda b, h, i: (b * nq + i, h))
    kvspec = pl.BlockSpec((SEQ, LANES), lambda b, h, i: (b, h))
    return pl.pallas_call(
        functools.partial(_mla_kernel, tq=tq, tk=tk),
        grid=(BATCH, MLA_HEADS, nq),
        in_specs=[qspec, kvspec, kvspec],
        out_specs=qspec,
        out_shape=jax.ShapeDtypeStruct((TOKENS, MLA_W), BF16),
        scratch_shapes=[pltpu.VMEM((tq, 1), F32), pltpu.VMEM((tq, 1), F32),
                        pltpu.VMEM((tq, LANES), F32)],
        compiler_params=_params("parallel", "parallel", "arbitrary"),
        name="mla_attn",
    )(q, k, v)


def _rope_tables(dtype):
    pos = jnp.arange(SEQ, dtype=F32)
    inv_freq = ROPE_THETA ** (-jnp.arange(0, MLA_ROPE, 2, dtype=F32) / MLA_ROPE)
    ang = pos[:, None] * inv_freq[None, :]
    cos, sin = jnp.cos(ang).astype(dtype), jnp.sin(ang).astype(dtype)
    zeros = jnp.zeros((SEQ, LANES - MLA_NOPE - MLA_ROPE), dtype)
    ctab = jnp.concatenate([jnp.ones((SEQ, MLA_NOPE), dtype), cos, cos, zeros], axis=1)
    stab = jnp.concatenate([jnp.zeros((SEQ, MLA_NOPE), dtype), -sin, sin, zeros], axis=1)
    return ctab, stab


def _swap_halves(w):
    half = w.shape[-1] // 2
    return jnp.concatenate([w[..., half:], w[..., :half]], axis=-1)


def _place_rope(w):
    rows = w.shape[0]
    return jnp.concatenate([jnp.zeros((rows, MLA_NOPE), w.dtype), w,
                            jnp.zeros((rows, LANES - MLA_NOPE - MLA_ROPE), w.dtype)], axis=1)


def _odd_weights(w_in, w_gate, w_uq, w_ukv):
    o = 0
    gq = w_in[:, o:o + GLA_QK]; o += GLA_QK
    gk = w_in[:, o:o + GLA_QK]; o += GLA_QK
    gv = w_in[:, o:o + GLA_VW]; o += GLA_VW
    g_low = w_in[:, o:o + GLA_GATE_RANK]; o += GLA_GATE_RANK
    gr = w_in[:, o:o + GLA_VW]; o += GLA_VW
    c_q = w_in[:, o:o + MLA_Q_RANK]; o += MLA_Q_RANK
    c_kv = w_in[:, o:o + MLA_KV_RANK]; o += MLA_KV_RANK
    k_r = w_in[:, o:o + MLA_ROPE]
    pad = jnp.zeros((D_MODEL, LANES - GLA_GATE_RANK), w_in.dtype)
    w_ext = jnp.concatenate([gq, gk, gv, gr, c_q, c_kv, g_low, pad,
                             _place_rope(k_r), _place_rope(_swap_halves(k_r))], axis=1)
    wgate = jnp.concatenate([w_gate, jnp.zeros((LANES - GLA_GATE_RANK, GLA_QK), w_gate.dtype)], axis=0)

    per_q = MLA_NOPE + MLA_ROPE
    qa, qb = [], []
    zq = jnp.zeros((MLA_Q_RANK, LANES - per_q), w_uq.dtype)
    for h in range(MLA_HEADS):
        nope = w_uq[:, h * per_q:h * per_q + MLA_NOPE]
        rope = w_uq[:, h * per_q + MLA_NOPE:(h + 1) * per_q]
        qa.append(jnp.concatenate([nope, rope, zq], axis=1))
        qb.append(_place_rope(_swap_halves(rope)))
    wq = jnp.concatenate(qa + qb, axis=1)

    per_kv = MLA_NOPE + MLA_V
    ks, vs = [], []
    zk = jnp.zeros((MLA_KV_RANK, LANES - MLA_NOPE), w_ukv.dtype)
    for h in range(MLA_HEADS):
        ks.append(jnp.concatenate([w_ukv[:, h * per_kv:h * per_kv + MLA_NOPE], zk], axis=1))
        vs.append(w_ukv[:, h * per_kv + MLA_NOPE:(h + 1) * per_kv])
    wkv = jnp.concatenate(ks + vs, axis=1)
    return w_ext.astype(BF16), wgate.astype(BF16), wq.astype(BF16), wkv.astype(BF16)


def kernel(x, ev_w_in, ev_conv_w, ev_sinks, ev_w_out, od_w_in, od_gla_w_gate, od_gla_b_gate,
           od_gla_norm_g, od_mla_q_norm_g, od_mla_w_uq, od_mla_kv_norm_g, od_mla_w_ukv, od_w_out,
           ffn_w_gate, ffn_w_up, ffn_w_down, ln_mix_g, ln_mix_b, ln_ffn_g, ln_ffn_b):
    assert x.shape == (BATCH, SEQ, D_MODEL) and x.dtype == F32
    ctab, stab = _rope_tables(x.dtype)
    pm = jnp.asarray(_gla_prefix_matrix(), BF16)
    mk = jnp.asarray(_gla_pair_masks(), F32)
    half = D_MODEL // 2
    h = x.reshape(TOKENS, D_MODEL)
    for layer in range(DEPTH):
        i = layer // 2
        if layer % 2 == 0:
            y1, y2 = _even_call(h, ev_w_in[i].astype(BF16), ev_conv_w[i], ev_sinks[i])
            w_out = ev_w_out[i]
        else:
            w_ext, wgate, wq, wkv = _odd_weights(od_w_in[i], od_gla_w_gate[i],
                                                 od_mla_w_uq[i], od_mla_w_ukv[i])
            gq, gk, gv, gr, la, mq, mk_, mv = _odd_proj_call(
                h, w_ext, wgate, od_gla_b_gate[i][None, :], od_mla_q_norm_g[i][None, :], wq,
                od_mla_kv_norm_g[i][None, :], wkv, ctab, stab)
            y1 = _gla_call(gq, gk, gv, gr, la, pm, mk, od_gla_norm_g[i][None, :])
            y2 = _mla_call(mq, mk_, mv)
            w_out = od_w_out[i]
        h = _post_call(h, y1, y2, w_out[:half].astype(BF16), w_out[half:].astype(BF16),
                       ln_mix_g[layer][None, :], ln_mix_b[layer][None, :],
                       ffn_w_gate[layer].astype(BF16), ffn_w_up[layer].astype(BF16),
                       ffn_w_down[layer].astype(BF16),
                       ln_ffn_g[layer][None, :], ln_ffn_b[layer][None, :])
    return h.reshape(BATCH, SEQ, D_MODEL)
```

```python
import functools

import numpy as np
import jax
import jax.numpy as jnp
from jax import lax
from jax.experimental import pallas as pl
from jax.experimental.pallas import tpu as pltpu

F32 = jnp.float32
BF16 = jnp.bfloat16

D_MODEL = 1024
BATCH = 2
SEQ = 16384
DEPTH = 4
TOKENS = BATCH * SEQ

CONV_WIDTH = 512
CONV_K = 3
SWA_HEADS = 8
SWA_KV_HEADS = 2
SWA_HEAD_DIM = 64
SWA_BLOCK = 128
GLA_HEADS = 4
GLA_DK = 64
GLA_DV = 128
GLA_GATE_RANK = 16
GLA_GATE_TAU = 16.0
GLA_CHUNK = 64
MLA_HEADS = 4
MLA_Q_RANK = 256
MLA_KV_RANK = 128
MLA_NOPE = 64
MLA_ROPE = 32
MLA_V = 128
ROPE_THETA = 10000.0
D_FF = 2816
ALPHA = (2.0 * DEPTH) ** 0.25
LN_EPS = 1e-5
RMS_EPS = 1e-6
LOG2E = 1.4426950408889634

LANES = 128
SUBLANES = 8
VMEM_LIMIT = 56 * 1024 * 1024

EVEN_IN = 3 * CONV_WIDTH + SWA_HEADS * SWA_HEAD_DIM + 2 * SWA_KV_HEADS * SWA_HEAD_DIM
GLA_QK = GLA_HEADS * GLA_DK
GLA_VW = GLA_HEADS * GLA_DV
MLA_W = MLA_HEADS * LANES
ODD_EXT = 2 * GLA_QK + 2 * GLA_VW + MLA_Q_RANK + MLA_KV_RANK + 3 * LANES
GLA_LEVELS = 6
GLA_PROWS = (2 + GLA_LEVELS) * GLA_CHUNK

TM_EVEN = 512
TM_ODD = 512
TM_POST = 256
TG_GLA = 512
TQ_MLA = 512
TK_MLA = 512


def _params(*sem):
    return pltpu.CompilerParams(dimension_semantics=sem, vmem_limit_bytes=VMEM_LIMIT)


def _const_spec(shape):
    return pl.BlockSpec(shape, lambda *_: (0,) * len(shape))


def _layer_norm(v, g, b):
    mu = jnp.mean(v, axis=-1, keepdims=True)
    d = v - mu
    var = jnp.mean(d * d, axis=-1, keepdims=True)
    return d * lax.rsqrt(var + LN_EPS) * g + b


def _dot(a, b):
    return jnp.dot(a, b, preferred_element_type=F32)


def _dot_nt(a, b):
    return lax.dot_general(a, b, (((1,), (1,)), ((), ())), preferred_element_type=F32)


def _dot_tn(a, b):
    return lax.dot_general(a, b, (((0,), (0,)), ((), ())), preferred_element_type=F32)


def _post_kernel(x_ref, y1_ref, y2_ref, wo1_ref, wo2_ref, g1_ref, b1_ref,
                 wg_ref, wu_ref, wd_ref, g2_ref, b2_ref, o_ref):
    y = _dot(y1_ref[...], wo1_ref[...]) + _dot(y2_ref[...], wo2_ref[...])
    x1 = _layer_norm(ALPHA * x_ref[...] + y, g1_ref[...], b1_ref[...])
    xb = x1.astype(BF16)
    gate = _dot(xb, wg_ref[...])
    up = _dot(xb, wu_ref[...])
    h = (gate * jax.nn.sigmoid(gate) * up).astype(BF16)
    f = _dot(h, wd_ref[...])
    o_ref[...] = _layer_norm(ALPHA * x1 + f, g2_ref[...], b2_ref[...])


def _post_call(x, y1, y2, wo1, wo2, g1, b1, wg, wu, wd, g2, b2):
    tm = TM_POST
    half = D_MODEL // 2
    row = lambda w: pl.BlockSpec((tm, w), lambda i: (i, 0))
    return pl.pallas_call(
        _post_kernel,
        grid=(TOKENS // tm,),
        in_specs=[row(D_MODEL), row(half), row(half),
                  _const_spec((half, D_MODEL)), _const_spec((half, D_MODEL)),
                  _const_spec((1, D_MODEL)), _const_spec((1, D_MODEL)),
                  _const_spec((D_MODEL, D_FF)), _const_spec((D_MODEL, D_FF)),
                  _const_spec((D_FF, D_MODEL)),
                  _const_spec((1, D_MODEL)), _const_spec((1, D_MODEL))],
        out_specs=row(D_MODEL),
        out_shape=jax.ShapeDtypeStruct((TOKENS, D_MODEL), F32),
        compiler_params=_params("parallel"),
        name="post_ffn",
    )(x, y1, y2, wo1, wo2, g1, b1, wg, wu, wd, g2, b2)


def _even_kernel(x_ref, win_ref, cw_ref, sink_ref, ya_ref, yb_ref, zc_ref, kvp_ref, *, tm):
    i = pl.program_id(1)

    @pl.when(i == 0)
    def _():
        zc_ref[...] = jnp.zeros_like(zc_ref)
        kvp_ref[...] = jnp.zeros_like(kvp_ref)

    u = _dot(x_ref[...].astype(BF16), win_ref[...])
    cw = CONV_WIDTH
    b_gate = u[:, 0:cw]
    z = u[:, cw:2 * cw] * u[:, 2 * cw:3 * cw]

    w0 = cw_ref[0:1, :]
    w1 = cw_ref[1:2, :]
    w2 = cw_ref[2:3, :]
    r1 = pltpu.roll(z, 1, 0)
    r2 = pltpu.roll(z, 2, 0)
    c8 = zc_ref[...]
    c1 = pltpu.roll(c8, 1, 0)
    c2 = pltpu.roll(c8, 2, 0)
    rid = lax.broadcasted_iota(jnp.int32, (SUBLANES, cw), 0)
    top1 = jnp.where(rid < 1, c1, r1[0:SUBLANES])
    top2 = jnp.where(rid < 2, c2, r2[0:SUBLANES])
    y_top = top2 * w0 + top1 * w1 + z[0:SUBLANES] * w2
    y_rest = r2[SUBLANES:] * w0 + r1[SUBLANES:] * w1 + z[SUBLANES:] * w2
    ya_ref[0:SUBLANES, :] = (b_gate[0:SUBLANES] * y_top).astype(BF16)
    ya_ref[SUBLANES:, :] = (b_gate[SUBLANES:] * y_rest).astype(BF16)
    zc_ref[...] = z[tm - SUBLANES:tm]

    qoff = 3 * cw
    koff = qoff + SWA_HEADS * SWA_HEAD_DIM
    voff = koff + SWA_KV_HEADS * SWA_HEAD_DIM
    q = u[:, qoff:koff] * (SWA_HEAD_DIM ** -0.5)
    k = u[:, koff:voff].astype(BF16)
    v = u[:, voff:voff + LANES].astype(BF16)
    blk = SWA_BLOCK
    r = lax.broadcasted_iota(jnp.int32, (blk, 2 * blk), 0)
    c = lax.broadcasted_iota(jnp.int32, (blk, 2 * blk), 1)
    own = (c >= blk) & ((c - blk) <= r)
    prev = (c < blk) & (c > r)
    prev_first = (c < blk) & (c > r + jnp.where(i > 0, 0, blk))
    lane = lax.broadcasted_iota(jnp.int32, (blk, LANES), 1)
    low_half = lane < SWA_HEAD_DIM
    group = SWA_HEADS // SWA_KV_HEADS
    for j in range(tm // blk):
        rows = slice(j * blk, (j + 1) * blk)
        if j == 0:
            k_prev = kvp_ref[:, 0:LANES]
            v_prev = kvp_ref[:, LANES:2 * LANES]
            mask = own | prev_first
        else:
            k_prev = k[(j - 1) * blk:j * blk]
            v_prev = v[(j - 1) * blk:j * blk]
            mask = own | prev
        kk = jnp.concatenate([k_prev, k[rows]], axis=0)
        vv = jnp.concatenate([v_prev, v[rows]], axis=0)
        for pair in range(SWA_HEADS // 2):
            qp = q[rows, pair * LANES:(pair + 1) * LANES]
            qp_sw = pltpu.roll(qp, SWA_HEAD_DIM, 1)
            g = (2 * pair) // group
            in_g = low_half if g == 0 else jnp.logical_not(low_half)
            outs = []
            for par in range(2):
                h = 2 * pair + par
                src = qp if par == g else qp_sw
                qh = jnp.where(in_g, src, 0.0).astype(BF16)
                s = _dot_nt(qh, kk)
                s = jnp.where(mask, s, -jnp.inf)
                sink = sink_ref[h]
                m = jnp.maximum(jnp.max(s, axis=-1, keepdims=True), sink)
                p = jnp.exp(s - m)
                denom = jnp.sum(p, axis=-1, keepdims=True) + jnp.exp(sink - m)
                o2 = _dot(p.astype(BF16), vv) / denom
                outs.append(o2)
            if g == 0:
                out = jnp.where(low_half, outs[0], pltpu.roll(outs[1], SWA_HEAD_DIM, 1))
            else:
                out = jnp.where(low_half, pltpu.roll(outs[0], SWA_HEAD_DIM, 1), outs[1])
            yb_ref[rows, pair * LANES:(pair + 1) * LANES] = out.astype(BF16)
    kvp_ref[:, 0:LANES] = k[tm - blk:tm]
    kvp_ref[:, LANES:2 * LANES] = v[tm - blk:tm]


def _even_call(x, w_in, conv_w, sinks):
    tm = TM_EVEN
    nt = SEQ // tm
    row = lambda w: pl.BlockSpec((tm, w), lambda b, i: (b * nt + i, 0))
    half = D_MODEL // 2
    return pl.pallas_call(
        functools.partial(_even_kernel, tm=tm),
        grid=(BATCH, nt),
        in_specs=[row(D_MODEL), _const_spec((D_MODEL, EVEN_IN)), _const_spec((CONV_K, CONV_WIDTH)),
                  pl.BlockSpec(memory_space=pltpu.SMEM)],
        out_specs=[row(half), row(half)],
        out_shape=[jax.ShapeDtypeStruct((TOKENS, half), BF16)] * 2,
        scratch_shapes=[pltpu.VMEM((SUBLANES, CONV_WIDTH), F32),
                        pltpu.VMEM((SWA_BLOCK, 2 * LANES), BF16)],
        compiler_params=_params("arbitrary", "arbitrary"),
        name="even_mixer",
    )(x, w_in, conv_w, sinks)


def _odd_proj_kernel(x_ref, win_ref, wgate_ref, bgate_ref, gq_ref, wq_ref, gkv_ref, wkv_ref,
                     ctab_ref, stab_ref,
                     gq_o, gk_o, gv_o, gr_o, la_o, qt_o, k_o, vt_o):
    u = _dot(x_ref[...].astype(BF16), win_ref[...])
    o = 0
    gq_o[...] = (u[:, o:o + GLA_QK] * (GLA_DK ** -0.5)).astype(BF16); o += GLA_QK
    gk_o[...] = u[:, o:o + GLA_QK].astype(BF16); o += GLA_QK
    gv_o[...] = u[:, o:o + GLA_VW].astype(BF16); o += GLA_VW
    gr_o[...] = u[:, o:o + GLA_VW].astype(BF16); o += GLA_VW
    c_q = u[:, o:o + MLA_Q_RANK]; o += MLA_Q_RANK
    c_kv = u[:, o:o + MLA_KV_RANK]; o += MLA_KV_RANK
    g_low = u[:, o:o + LANES]; o += LANES
    kr = u[:, o:o + LANES]; o += LANES
    kr_sw = u[:, o:o + LANES]

    xg = _dot(g_low.astype(BF16), wgate_ref[...]) + bgate_ref[...]
    log_sig = jnp.minimum(xg, 0.0) - jnp.log(1.0 + jnp.exp(-jnp.abs(xg)))
    la_o[...] = log_sig * (1.0 / GLA_GATE_TAU)

    ctab = ctab_ref[...]
    stab = stab_ref[...]
    cqn = c_q * lax.rsqrt(jnp.mean(c_q * c_q, axis=-1, keepdims=True) + RMS_EPS) * gq_ref[...]
    q2 = _dot(cqn.astype(BF16), wq_ref[...])
    scale = (MLA_NOPE + MLA_ROPE) ** -0.5 * LOG2E
    ckvn = c_kv * lax.rsqrt(jnp.mean(c_kv * c_kv, axis=-1, keepdims=True) + RMS_EPS) * gkv_ref[...]
    kv = _dot(ckvn.astype(BF16), wkv_ref[...])
    k_rope = kr * ctab + kr_sw * stab
    for h in range(MLA_HEADS):
        cols = slice(h * LANES, (h + 1) * LANES)
        qa = q2[:, cols]
        qb = q2[:, MLA_W + h * LANES:MLA_W + (h + 1) * LANES]
        qt_o[0, cols, :] = ((qa * ctab + qb * stab) * scale).T.astype(BF16)
        k_o[:, cols] = (kv[:, cols] + k_rope).astype(BF16)
        vt_o[0, cols, :] = kv[:, MLA_W + h * LANES:MLA_W + (h + 1) * LANES].T.astype(BF16)


def _odd_proj_call(x, w_in, wgate, bgate, gq, wq, gkv, wkv, ctab, stab):
    tm = TM_ODD
    nt = SEQ // tm
    row = lambda w: pl.BlockSpec((tm, w), lambda i: (i, 0))
    tab = pl.BlockSpec((tm, LANES), lambda i: (i % nt, 0))
    tile_t = pl.BlockSpec((1, MLA_W, tm), lambda i: (i, 0, 0))
    rows_out = [(GLA_QK, BF16), (GLA_QK, BF16), (GLA_VW, BF16), (GLA_VW, BF16), (GLA_QK, F32)]
    row_shape = lambda w, dt: jax.ShapeDtypeStruct((TOKENS, w), dt)
    tile_shape = jax.ShapeDtypeStruct((TOKENS // tm, MLA_W, tm), BF16)
    return pl.pallas_call(
        _odd_proj_kernel,
        grid=(TOKENS // tm,),
        in_specs=[row(D_MODEL), _const_spec((D_MODEL, ODD_EXT)),
                  _const_spec((LANES, GLA_QK)), _const_spec((1, GLA_QK)),
                  _const_spec((1, MLA_Q_RANK)), _const_spec((MLA_Q_RANK, 2 * MLA_W)),
                  _const_spec((1, MLA_KV_RANK)), _const_spec((MLA_KV_RANK, 2 * MLA_W)),
                  tab, tab],
        out_specs=[row(w) for w, _ in rows_out] + [tile_t, row(MLA_W), tile_t],
        out_shape=[row_shape(w, dt) for w, dt in rows_out] + [tile_shape, row_shape(MLA_W, BF16), tile_shape],
        compiler_params=_params("parallel"),
        name="odd_proj",
    )(x, w_in, wgate, bgate, gq, wq, gkv, wkv, ctab, stab)


def _gla_prefix_matrix():
    c = GLA_CHUNK
    t = np.arange(c)[:, None]
    u = np.arange(c)[None, :]
    blocks = [(u <= t), (u > t)]
    length = c // 2
    while length >= 1:
        rho = (t // (2 * length)) * 2 * length + length - 1
        right = (t % (2 * length)) >= length
        blocks.append(np.where(right, (u > rho) & (u <= t), (u > t) & (u <= rho)))
        length //= 2
    return np.concatenate(blocks, axis=0).astype(np.float32)


def _gla_pair_masks():
    c = GLA_CHUNK
    t = np.arange(c)[:, None]
    s = np.arange(c)[None, :]
    masks = []
    length = c // 2
    while length >= 1:
        same = (t // (2 * length)) == (s // (2 * length))
        masks.append(same & ((t % (2 * length)) >= length) & ((s % (2 * length)) < length))
        length //= 2
    masks.append(t == s)
    m = np.stack(masks).astype(np.float32)
    return np.tile(m, (1, 1, GLA_HEADS))


def _gla_kernel(q_ref, k_ref, v_ref, r_ref, g_ref, pm_ref, mk_ref, gn_ref, o_ref, st_ref, *, tg):
    @pl.when(pl.program_id(1) == 0)
    def _():
        st_ref[...] = jnp.zeros_like(st_ref)

    c = GLA_CHUNK
    nh = GLA_HEADS
    lane_qk = lax.broadcasted_iota(jnp.int32, (c, GLA_QK), 1)
    head_of_lane = [(lane_qk >= h * GLA_DK) & (lane_qk < (h + 1) * GLA_DK) for h in range(nh)]
    zeros_v = jnp.zeros((c, GLA_DV), BF16)
    ones_c = jnp.ones((c, LANES), BF16)
    gnorm = gn_ref[...]

    def stack_heads(a):
        return jnp.concatenate([jnp.where(head_of_lane[h], a, 0.0) for h in range(nh)],
                               axis=0).astype(BF16)

    def block_diag(blocks):
        rows = []
        for h in range(nh):
            rows.append(jnp.concatenate(
                [blocks[h] if hh == h else zeros_v for hh in range(nh)], axis=1))
        return jnp.concatenate(rows, axis=0)

    def chunk(ci, carry):
        r0 = pl.multiple_of(ci * c, c)
        rows = pl.ds(r0, c)
        q = q_ref[rows, :].astype(F32)
        k = k_ref[rows, :].astype(F32)
        v = v_ref[rows, :]
        g = g_ref[rows, :]
        g_hi = g.astype(BF16)
        g_lo = (g - g_hi.astype(F32)).astype(BF16)
        g2 = jnp.concatenate([g_hi, g_lo], axis=1)
        x2 = _dot(pm_ref[...], g2)
        xs = x2[:, 0:GLA_QK] + x2[:, GLA_QK:2 * GLA_QK]
        b = xs[0:c]
        qe = (q * jnp.exp(b)).astype(BF16)
        kd = (k * jnp.exp(xs[c:2 * c])).astype(BF16)

        att = _dot_nt(q.astype(BF16), stack_heads(k)) * mk_ref[GLA_LEVELS]
        for lv in range(GLA_LEVELS):
            e = jnp.exp(xs[(2 + lv) * c:(3 + lv) * c])
            att = att + _dot_nt((q * e).astype(BF16), stack_heads(k * e)) * mk_ref[lv]

        v_heads = [v[:, h * GLA_DV:(h + 1) * GLA_DV] for h in range(nh)]
        s_heads = [st_ref[h].astype(BF16) for h in range(nh)]
        lhs = jnp.concatenate([att.astype(BF16), qe], axis=1)
        rhs = jnp.concatenate([block_diag(v_heads), block_diag(s_heads)], axis=0)
        o = _dot(lhs, rhs)

        upd = _dot_tn(kd, v)
        bcol2 = _dot_tn(g2, ones_c)
        dcol = jnp.exp(bcol2[0:GLA_QK] + bcol2[GLA_QK:2 * GLA_QK])
        for h in range(nh):
            rs = slice(h * GLA_DK, (h + 1) * GLA_DK)
            st_ref[h] = dcol[rs] * st_ref[h] + upd[rs, h * GLA_DV:(h + 1) * GLA_DV]

        rr = r_ref[rows, :].astype(F32)
        gate = rr * jax.nn.sigmoid(rr)
        for h in range(nh):
            cs = slice(h * GLA_DV, (h + 1) * GLA_DV)
            oh = o[:, cs]
            ms = jnp.mean(oh * oh, axis=-1, keepdims=True)
            o_ref[rows, cs] = (oh * lax.rsqrt(ms + RMS_EPS) * gnorm * gate[:, cs]).astype(BF16)
        return carry

    lax.fori_loop(0, tg // c, chunk, 0)


def _gla_call(gq, gk, gv, gr, la, pm, mk, gn):
    tg = TG_GLA
    nt = SEQ // tg
    row = lambda w: pl.BlockSpec((tg, w), lambda b, i: (b * nt + i, 0))
    return pl.pallas_call(
        functools.partial(_gla_kernel, tg=tg),
        grid=(BATCH, nt),
        in_specs=[row(GLA_QK), row(GLA_QK), row(GLA_VW), row(GLA_VW), row(GLA_QK),
                  _const_spec((GLA_PROWS, GLA_CHUNK)),
                  _const_spec((GLA_LEVELS + 1, GLA_CHUNK, GLA_HEADS * GLA_CHUNK)),
                  _const_spec((1, GLA_DV))],
        out_specs=row(GLA_VW),
        out_shape=jax.ShapeDtypeStruct((TOKENS, GLA_VW), BF16),
        scratch_shapes=[pltpu.VMEM((GLA_HEADS, GLA_DK, GLA_DV), F32)],
        compiler_params=_params("arbitrary", "arbitrary"),
        name="gla_mixer",
    )(gq, gk, gv, gr, la, pm, mk, gn)


def _mla_kernel(qt_ref, k_ref, vt_ref, o_ref, m_ref, l_ref, acc_ref, *, tq, tk):
    qi = pl.program_id(1)
    ratio = tk // tq
    m_ref[...] = jnp.full_like(m_ref, -jnp.inf)
    l_ref[...] = jnp.zeros_like(l_ref)
    acc_ref[...] = jnp.zeros_like(acc_ref)

    def step(j, r0, nrows, masked):
        k0 = pl.multiple_of(j * tk + r0, tq)
        heads = range(MLA_HEADS)
        cols = [slice(h * LANES, (h + 1) * LANES) for h in heads]
        sts = [_dot(k_ref[pl.ds(k0, nrows), cols[h]], qt_ref[0, cols[h], :]) for h in heads]
        if masked:
            r = lax.broadcasted_iota(jnp.int32, (nrows, tq), 0)
            c = lax.broadcasted_iota(jnp.int32, (nrows, tq), 1)
            sts = [jnp.where(r <= c, st, -jnp.inf) for st in sts]
        m_prev = [m_ref[h] for h in heads]
        m_new = [jnp.maximum(m_prev[h], jnp.max(sts[h], axis=0, keepdims=True)) for h in heads]
        ps = [jnp.exp2(sts[h] - m_new[h]) for h in heads]
        pvs = [_dot(vt_ref[j, cols[h], r0:r0 + nrows], ps[h].astype(BF16)) for h in heads]
        for h in heads:
            a = jnp.exp2(m_prev[h] - m_new[h])
            l_ref[h] = a * l_ref[h] + jnp.sum(ps[h], axis=0, keepdims=True)
            acc_ref[h] = a * acc_ref[h] + pvs[h]
            m_ref[h] = m_new[h]

    def body(j, carry):
        step(j, 0, tk, False)
        return carry

    jd = qi // ratio
    lax.fori_loop(0, jd, body, 0)

    for piece in range(ratio):
        @pl.when(qi % ratio == piece)
        def _():
            for before in range(piece):
                step(jd, before * tq, tq, False)
            step(jd, piece * tq, tq, True)

    for h in range(MLA_HEADS):
        cols = slice(h * LANES, (h + 1) * LANES)
        o_ref[:, cols] = (acc_ref[h] / l_ref[h]).T.astype(BF16)


def _mla_call(qt, k, vt):
    tq, tk = TQ_MLA, TK_MLA
    assert tk == TM_ODD and tk % tq == 0
    ratio = tk // tq
    nq = SEQ // tq
    nk = SEQ // tk
    once = pl.Buffered(1)
    return pl.pallas_call(
        functools.partial(_mla_kernel, tq=tq, tk=tk),
        grid=(BATCH, nq),
        in_specs=[pl.BlockSpec((1, MLA_W, tq), lambda b, i: (b * nk + i // ratio, 0, i % ratio)),
                  pl.BlockSpec((SEQ, MLA_W), lambda b, i: (b, 0), pipeline_mode=once),
                  pl.BlockSpec((nk, MLA_W, tk), lambda b, i: (b, 0, 0), pipeline_mode=once)],
        out_specs=pl.BlockSpec((tq, MLA_W), lambda b, i: (b * nq + i, 0)),
        out_shape=jax.ShapeDtypeStruct((TOKENS, MLA_W), BF16),
        scratch_shapes=[pltpu.VMEM((MLA_HEADS, 1, tq), F32), pltpu.VMEM((MLA_HEADS, 1, tq), F32),
                        pltpu.VMEM((MLA_HEADS, LANES, tq), F32)],
        compiler_params=_params("parallel", "arbitrary"),
        name="mla_attn",
    )(qt, k, vt)


def _rope_tables(dtype):
    pos = jnp.arange(SEQ, dtype=F32)
    inv_freq = ROPE_THETA ** (-jnp.arange(0, MLA_ROPE, 2, dtype=F32) / MLA_ROPE)
    ang = pos[:, None] * inv_freq[None, :]
    cos, sin = jnp.cos(ang).astype(dtype), jnp.sin(ang).astype(dtype)
    zeros = jnp.zeros((SEQ, LANES - MLA_NOPE - MLA_ROPE), dtype)
    ctab = jnp.concatenate([jnp.ones((SEQ, MLA_NOPE), dtype), cos, cos, zeros], axis=1)
    stab = jnp.concatenate([jnp.zeros((SEQ, MLA_NOPE), dtype), -sin, sin, zeros], axis=1)
    return ctab, stab


def _swap_halves(w):
    half = w.shape[-1] // 2
    return jnp.concatenate([w[..., half:], w[..., :half]], axis=-1)


def _place_rope(w):
    rows = w.shape[0]
    return jnp.concatenate([jnp.zeros((rows, MLA_NOPE), w.dtype), w,
                            jnp.zeros((rows, LANES - MLA_NOPE - MLA_ROPE), w.dtype)], axis=1)


def _odd_weights(w_in, w_gate, w_uq, w_ukv):
    o = 0
    gq = w_in[:, o:o + GLA_QK]; o += GLA_QK
    gk = w_in[:, o:o + GLA_QK]; o += GLA_QK
    gv = w_in[:, o:o + GLA_VW]; o += GLA_VW
    g_low = w_in[:, o:o + GLA_GATE_RANK]; o += GLA_GATE_RANK
    gr = w_in[:, o:o + GLA_VW]; o += GLA_VW
    c_q = w_in[:, o:o + MLA_Q_RANK]; o += MLA_Q_RANK
    c_kv = w_in[:, o:o + MLA_KV_RANK]; o += MLA_KV_RANK
    k_r = w_in[:, o:o + MLA_ROPE]
    pad = jnp.zeros((D_MODEL, LANES - GLA_GATE_RANK), w_in.dtype)
    w_ext = jnp.concatenate([gq, gk, gv, gr, c_q, c_kv, g_low, pad,
                             _place_rope(k_r), _place_rope(_swap_halves(k_r))], axis=1)
    wgate = jnp.concatenate([w_gate, jnp.zeros((LANES - GLA_GATE_RANK, GLA_QK), w_gate.dtype)], axis=0)

    per_q = MLA_NOPE + MLA_ROPE
    qa, qb = [], []
    zq = jnp.zeros((MLA_Q_RANK, LANES - per_q), w_uq.dtype)
    for h in range(MLA_HEADS):
        nope = w_uq[:, h * per_q:h * per_q + MLA_NOPE]
        rope = w_uq[:, h * per_q + MLA_NOPE:(h + 1) * per_q]
        qa.append(jnp.concatenate([nope, rope, zq], axis=1))
        qb.append(_place_rope(_swap_halves(rope)))
    wq = jnp.concatenate(qa + qb, axis=1)

    per_kv = MLA_NOPE + MLA_V
    ks, vs = [], []
    zk = jnp.zeros((MLA_KV_RANK, LANES - MLA_NOPE), w_ukv.dtype)
    for h in range(MLA_HEADS):
        ks.append(jnp.concatenate([w_ukv[:, h * per_kv:h * per_kv + MLA_NOPE], zk], axis=1))
        vs.append(w_ukv[:, h * per_kv + MLA_NOPE:(h + 1) * per_kv])
    wkv = jnp.concatenate(ks + vs, axis=1)
    return w_ext.astype(BF16), wgate.astype(BF16), wq.astype(BF16), wkv.astype(BF16)


def kernel(x, ev_w_in, ev_conv_w, ev_sinks, ev_w_out, od_w_in, od_gla_w_gate, od_gla_b_gate,
           od_gla_norm_g, od_mla_q_norm_g, od_mla_w_uq, od_mla_kv_norm_g, od_mla_w_ukv, od_w_out,
           ffn_w_gate, ffn_w_up, ffn_w_down, ln_mix_g, ln_mix_b, ln_ffn_g, ln_ffn_b):
    assert x.shape == (BATCH, SEQ, D_MODEL) and x.dtype == F32
    ctab, stab = _rope_tables(x.dtype)
    pm = jnp.asarray(_gla_prefix_matrix(), BF16)
    mk = jnp.asarray(_gla_pair_masks(), F32)
    half = D_MODEL // 2
    h = x.reshape(TOKENS, D_MODEL)
    for layer in range(DEPTH):
        i = layer // 2
        if layer % 2 == 0:
            y1, y2 = _even_call(h, ev_w_in[i].astype(BF16), ev_conv_w[i], ev_sinks[i])
            w_out = ev_w_out[i]
        else:
            w_ext, wgate, wq, wkv = _odd_weights(od_w_in[i], od_gla_w_gate[i],
                                                 od_mla_w_uq[i], od_mla_w_ukv[i])
            gq, gk, gv, gr, la, mq, mk_, mv = _odd_proj_call(
                h, w_ext, wgate, od_gla_b_gate[i][None, :], od_mla_q_norm_g[i][None, :], wq,
                od_mla_kv_norm_g[i][None, :], wkv, ctab, stab)
            y1 = _gla_call(gq, gk, gv, gr, la, pm, mk, od_gla_norm_g[i][None, :])
            y2 = _mla_call(mq, mk_, mv)
            w_out = od_w_out[i]
        h = _post_call(h, y1, y2, w_out[:half].astype(BF16), w_out[half:].astype(BF16),
                       ln_mix_g[layer][None, :], ln_mix_b[layer][None, :],
                       ffn_w_gate[layer].astype(BF16), ffn_w_up[layer].astype(BF16),
                       ffn_w_down[layer].astype(BF16),
                       ln_ffn_g[layer][None, :], ln_ffn_b[layer][None, :])
    return h.reshape(BATCH, SEQ, D_MODEL)
```

```python
import functools

import numpy as np
import jax
import jax.numpy as jnp
from jax import lax
from jax.experimental import pallas as pl
from jax.experimental.pallas import tpu as pltpu

F32 = jnp.float32
BF16 = jnp.bfloat16

D_MODEL = 1024
BATCH = 2
SEQ = 16384
DEPTH = 4
TOKENS = BATCH * SEQ

CONV_WIDTH = 512
CONV_K = 3
SWA_HEADS = 8
SWA_KV_HEADS = 2
SWA_HEAD_DIM = 64
SWA_BLOCK = 128
GLA_HEADS = 4
GLA_DK = 64
GLA_DV = 128
GLA_GATE_RANK = 16
GLA_GATE_TAU = 16.0
GLA_CHUNK = 64
MLA_HEADS = 4
MLA_Q_RANK = 256
MLA_KV_RANK = 128
MLA_NOPE = 64
MLA_ROPE = 32
MLA_V = 128
ROPE_THETA = 10000.0
D_FF = 2816
ALPHA = (2.0 * DEPTH) ** 0.25
LN_EPS = 1e-5
RMS_EPS = 1e-6
LOG2E = 1.4426950408889634

LANES = 128
SUBLANES = 8
VMEM_LIMIT = 56 * 1024 * 1024

EVEN_IN = 3 * CONV_WIDTH + SWA_HEADS * SWA_HEAD_DIM + 2 * SWA_KV_HEADS * SWA_HEAD_DIM
GLA_QK = GLA_HEADS * GLA_DK
GLA_VW = GLA_HEADS * GLA_DV
MLA_W = MLA_HEADS * LANES
MLA_VT_ROWS = MLA_V + 16
ODD_EXT = 2 * GLA_QK + 2 * GLA_VW + MLA_Q_RANK + MLA_KV_RANK + 3 * LANES
GLA_LEVELS = 6
GLA_PROWS = (2 + GLA_LEVELS) * GLA_CHUNK

TM_EVEN = 512
TM_ODD = 512
TM_POST = 256
TG_GLA = 512
TQ_MLA = 512
MLA_ROW_CHUNK = 64
MLA_RING = 4
MLA_M_INIT = -3.0e38
TK_MLA = 512


def _params(*sem):
    return pltpu.CompilerParams(dimension_semantics=sem, vmem_limit_bytes=VMEM_LIMIT)


def _const_spec(shape):
    return pl.BlockSpec(shape, lambda *_: (0,) * len(shape))


def _layer_norm(v, g, b):
    mu = jnp.mean(v, axis=-1, keepdims=True)
    d = v - mu
    var = jnp.mean(d * d, axis=-1, keepdims=True)
    return d * lax.rsqrt(var + LN_EPS) * g + b


def _dot(a, b):
    return jnp.dot(a, b, preferred_element_type=F32)


def _dot_nt(a, b):
    return lax.dot_general(a, b, (((1,), (1,)), ((), ())), preferred_element_type=F32)


def _dot_tn(a, b):
    return lax.dot_general(a, b, (((0,), (0,)), ((), ())), preferred_element_type=F32)


def _post_kernel(x_ref, y1_ref, y2_ref, wo1_ref, wo2_ref, g1_ref, b1_ref,
                 wg_ref, wu_ref, wd_ref, g2_ref, b2_ref, o_ref):
    y = _dot(y1_ref[...], wo1_ref[...]) + _dot(y2_ref[...], wo2_ref[...])
    x1 = _layer_norm(ALPHA * x_ref[...] + y, g1_ref[...], b1_ref[...])
    xb = x1.astype(BF16)
    gate = _dot(xb, wg_ref[...])
    up = _dot(xb, wu_ref[...])
    h = (gate * jax.nn.sigmoid(gate) * up).astype(BF16)
    f = _dot(h, wd_ref[...])
    o_ref[...] = _layer_norm(ALPHA * x1 + f, g2_ref[...], b2_ref[...])


def _post_call(x, y1, y2, wo1, wo2, g1, b1, wg, wu, wd, g2, b2):
    tm = TM_POST
    half = D_MODEL // 2
    row = lambda w: pl.BlockSpec((tm, w), lambda i: (i, 0))
    return pl.pallas_call(
        _post_kernel,
        grid=(TOKENS // tm,),
        in_specs=[row(D_MODEL), row(half), row(half),
                  _const_spec((half, D_MODEL)), _const_spec((half, D_MODEL)),
                  _const_spec((1, D_MODEL)), _const_spec((1, D_MODEL)),
                  _const_spec((D_MODEL, D_FF)), _const_spec((D_MODEL, D_FF)),
                  _const_spec((D_FF, D_MODEL)),
                  _const_spec((1, D_MODEL)), _const_spec((1, D_MODEL))],
        out_specs=row(D_MODEL),
        out_shape=jax.ShapeDtypeStruct((TOKENS, D_MODEL), F32),
        compiler_params=_params("parallel"),
        name="post_ffn",
    )(x, y1, y2, wo1, wo2, g1, b1, wg, wu, wd, g2, b2)


def _even_kernel(x_ref, win_ref, cw_ref, sink_ref, ya_ref, yb_ref, zc_ref, kvp_ref, *, tm):
    i = pl.program_id(1)

    @pl.when(i == 0)
    def _():
        zc_ref[...] = jnp.zeros_like(zc_ref)
        kvp_ref[...] = jnp.zeros_like(kvp_ref)

    u = _dot(x_ref[...].astype(BF16), win_ref[...])
    cw = CONV_WIDTH
    b_gate = u[:, 0:cw]
    z = u[:, cw:2 * cw] * u[:, 2 * cw:3 * cw]

    w0 = cw_ref[0:1, :]
    w1 = cw_ref[1:2, :]
    w2 = cw_ref[2:3, :]
    r1 = pltpu.roll(z, 1, 0)
    r2 = pltpu.roll(z, 2, 0)
    c8 = zc_ref[...]
    c1 = pltpu.roll(c8, 1, 0)
    c2 = pltpu.roll(c8, 2, 0)
    rid = lax.broadcasted_iota(jnp.int32, (SUBLANES, cw), 0)
    top1 = jnp.where(rid < 1, c1, r1[0:SUBLANES])
    top2 = jnp.where(rid < 2, c2, r2[0:SUBLANES])
    y_top = top2 * w0 + top1 * w1 + z[0:SUBLANES] * w2
    y_rest = r2[SUBLANES:] * w0 + r1[SUBLANES:] * w1 + z[SUBLANES:] * w2
    ya_ref[0:SUBLANES, :] = (b_gate[0:SUBLANES] * y_top).astype(BF16)
    ya_ref[SUBLANES:, :] = (b_gate[SUBLANES:] * y_rest).astype(BF16)
    zc_ref[...] = z[tm - SUBLANES:tm]

    qoff = 3 * cw
    koff = qoff + SWA_HEADS * SWA_HEAD_DIM
    voff = koff + SWA_KV_HEADS * SWA_HEAD_DIM
    q = u[:, qoff:koff] * (SWA_HEAD_DIM ** -0.5)
    k = u[:, koff:voff].astype(BF16)
    v = u[:, voff:voff + LANES].astype(BF16)
    blk = SWA_BLOCK
    r = lax.broadcasted_iota(jnp.int32, (blk, 2 * blk), 0)
    c = lax.broadcasted_iota(jnp.int32, (blk, 2 * blk), 1)
    own = (c >= blk) & ((c - blk) <= r)
    prev = (c < blk) & (c > r)
    prev_first = (c < blk) & (c > r + jnp.where(i > 0, 0, blk))
    lane = lax.broadcasted_iota(jnp.int32, (blk, LANES), 1)
    low_half = lane < SWA_HEAD_DIM
    group = SWA_HEADS // SWA_KV_HEADS
    mask_inner = own | prev
    mask_first = own | prev_first
    nblk = tm // blk

    def block_scores(j):
        rows = slice(j * blk, (j + 1) * blk)
        if j == 0:
            k_prev = kvp_ref[:, 0:LANES]
            v_prev = kvp_ref[:, LANES:2 * LANES]
        else:
            k_prev = k[(j - 1) * blk:j * blk]
            v_prev = v[(j - 1) * blk:j * blk]
        kk = jnp.concatenate([k_prev, k[rows]], axis=0)
        vv = jnp.concatenate([v_prev, v[rows]], axis=0)
        scores = []
        for g in range(SWA_KV_HEADS):
            in_g = low_half if g == 0 else jnp.logical_not(low_half)
            qs = []
            for i in range(group):
                h = g * group + i
                qp = q[rows, (h // 2) * LANES:(h // 2 + 1) * LANES]
                src = qp if h % 2 == g else pltpu.roll(qp, SWA_HEAD_DIM, 1)
                qs.append(jnp.where(in_g, src, 0.0).astype(BF16))
            scores.append(_dot_nt(jnp.concatenate(qs, axis=0), kk))
        return scores, vv

    def block_finish(j, scores, vv):
        rows = slice(j * blk, (j + 1) * blk)
        mask = mask_first if j == 0 else mask_inner
        ps, dens = [], []
        for g in range(SWA_KV_HEADS):
            ps_g, dens_g = [], []
            for i in range(group):
                sink = sink_ref[g * group + i]
                s = jnp.where(mask, scores[g][i * blk:(i + 1) * blk], -jnp.inf)
                m = jnp.maximum(jnp.max(s, axis=-1, keepdims=True), sink)
                p = jnp.exp(s - m)
                dens_g.append(jnp.sum(p, axis=-1, keepdims=True) + jnp.exp(sink - m))
                ps_g.append(p.astype(BF16))
            ps.append(jnp.concatenate(ps_g, axis=0))
            dens.append(dens_g)
        outs = [_dot(ps[g], vv) for g in range(SWA_KV_HEADS)]
        for g in range(SWA_KV_HEADS):
            for pair in range(group // 2):
                even = outs[g][(2 * pair) * blk:(2 * pair + 1) * blk] / dens[g][2 * pair]
                odd = outs[g][(2 * pair + 1) * blk:(2 * pair + 2) * blk] / dens[g][2 * pair + 1]
                if g == 0:
                    out = jnp.where(low_half, even, pltpu.roll(odd, SWA_HEAD_DIM, 1))
                else:
                    out = jnp.where(low_half, pltpu.roll(even, SWA_HEAD_DIM, 1), odd)
                col = (g * group // 2 + pair) * LANES
                yb_ref[rows, col:col + LANES] = out.astype(BF16)

    pending = None
    for j in range(nblk + 1):
        current = block_scores(j) if j < nblk else None
        if pending is not None:
            block_finish(j - 1, *pending)
        pending = current
    kvp_ref[:, 0:LANES] = k[tm - blk:tm]
    kvp_ref[:, LANES:2 * LANES] = v[tm - blk:tm]


def _even_call(x, w_in, conv_w, sinks):
    tm = TM_EVEN
    nt = SEQ // tm
    row = lambda w: pl.BlockSpec((tm, w), lambda b, i: (b * nt + i, 0))
    half = D_MODEL // 2
    return pl.pallas_call(
        functools.partial(_even_kernel, tm=tm),
        grid=(BATCH, nt),
        in_specs=[row(D_MODEL), _const_spec((D_MODEL, EVEN_IN)), _const_spec((CONV_K, CONV_WIDTH)),
                  pl.BlockSpec(memory_space=pltpu.SMEM)],
        out_specs=[row(half), row(half)],
        out_shape=[jax.ShapeDtypeStruct((TOKENS, half), BF16)] * 2,
        scratch_shapes=[pltpu.VMEM((SUBLANES, CONV_WIDTH), F32),
                        pltpu.VMEM((SWA_BLOCK, 2 * LANES), BF16)],
        compiler_params=_params("arbitrary", "arbitrary"),
        name="even_mixer",
    )(x, w_in, conv_w, sinks)


def _odd_proj_kernel(x_ref, win_ref, wgate_ref, bgate_ref, gq_ref, wq_ref, gkv_ref, wkv_ref,
                     ctab_ref, stab_ref,
                     gq_o, gk_o, gv_o, gr_o, la_o, qt_o, k_o, vt_o):
    u = _dot(x_ref[...].astype(BF16), win_ref[...])
    o = 0
    gq_o[...] = (u[:, o:o + GLA_QK] * (GLA_DK ** -0.5)).astype(BF16); o += GLA_QK
    gk_o[...] = u[:, o:o + GLA_QK].astype(BF16); o += GLA_QK
    gv_o[...] = u[:, o:o + GLA_VW].astype(BF16); o += GLA_VW
    gr_o[...] = u[:, o:o + GLA_VW].astype(BF16); o += GLA_VW
    c_q = u[:, o:o + MLA_Q_RANK]; o += MLA_Q_RANK
    c_kv = u[:, o:o + MLA_KV_RANK]; o += MLA_KV_RANK
    g_low = u[:, o:o + LANES]; o += LANES
    kr = u[:, o:o + LANES]; o += LANES
    kr_sw = u[:, o:o + LANES]

    xg = _dot(g_low.astype(BF16), wgate_ref[...]) + bgate_ref[...]
    log_sig = jnp.minimum(xg, 0.0) - jnp.log(1.0 + jnp.exp(-jnp.abs(xg)))
    la_o[...] = log_sig * (1.0 / GLA_GATE_TAU)

    ctab = ctab_ref[...]
    stab = stab_ref[...]
    cqn = c_q * lax.rsqrt(jnp.mean(c_q * c_q, axis=-1, keepdims=True) + RMS_EPS) * gq_ref[...]
    q2 = _dot(cqn.astype(BF16), wq_ref[...])
    scale = (MLA_NOPE + MLA_ROPE) ** -0.5 * LOG2E
    ckvn = c_kv * lax.rsqrt(jnp.mean(c_kv * c_kv, axis=-1, keepdims=True) + RMS_EPS) * gkv_ref[...]
    kv = _dot(ckvn.astype(BF16), wkv_ref[...])
    k_rope = kr * ctab + kr_sw * stab
    for h in range(MLA_HEADS):
        cols = slice(h * LANES, (h + 1) * LANES)
        qa = q2[:, cols]
        qb = q2[:, MLA_W + h * LANES:MLA_W + (h + 1) * LANES]
        qt_o[0, cols, :] = ((qa * ctab + qb * stab) * scale).T.astype(BF16)
        k_o[:, cols] = (kv[:, cols] + k_rope).astype(BF16)
        vrow = h * MLA_VT_ROWS
        vt_o[0, vrow:vrow + MLA_V, :] = kv[:, MLA_W + h * LANES:MLA_W + (h + 1) * LANES].T.astype(BF16)
        vt_o[0, vrow + MLA_V:vrow + MLA_VT_ROWS, :] = jnp.ones((MLA_VT_ROWS - MLA_V, x_ref.shape[0]), BF16)


def _odd_proj_call(x, w_in, wgate, bgate, gq, wq, gkv, wkv, ctab, stab):
    tm = TM_ODD
    nt = SEQ // tm
    row = lambda w: pl.BlockSpec((tm, w), lambda i: (i, 0))
    tab = pl.BlockSpec((tm, LANES), lambda i: (i % nt, 0))
    tile_t = lambda r: pl.BlockSpec((1, r, tm), lambda i: (i, 0, 0))
    rows_out = [(GLA_QK, BF16), (GLA_QK, BF16), (GLA_VW, BF16), (GLA_VW, BF16), (GLA_QK, F32)]
    row_shape = lambda w, dt: jax.ShapeDtypeStruct((TOKENS, w), dt)
    tile_shape = lambda r: jax.ShapeDtypeStruct((TOKENS // tm, r, tm), BF16)
    vt_rows = MLA_HEADS * MLA_VT_ROWS
    return pl.pallas_call(
        _odd_proj_kernel,
        grid=(TOKENS // tm,),
        in_specs=[row(D_MODEL), _const_spec((D_MODEL, ODD_EXT)),
                  _const_spec((LANES, GLA_QK)), _const_spec((1, GLA_QK)),
                  _const_spec((1, MLA_Q_RANK)), _const_spec((MLA_Q_RANK, 2 * MLA_W)),
                  _const_spec((1, MLA_KV_RANK)), _const_spec((MLA_KV_RANK, 2 * MLA_W)),
                  tab, tab],
        out_specs=[row(w) for w, _ in rows_out] + [tile_t(MLA_W), row(MLA_W), tile_t(vt_rows)],
        out_shape=[row_shape(w, dt) for w, dt in rows_out]
        + [tile_shape(MLA_W), row_shape(MLA_W, BF16), tile_shape(vt_rows)],
        compiler_params=_params("parallel"),
        name="odd_proj",
    )(x, w_in, wgate, bgate, gq, wq, gkv, wkv, ctab, stab)


def _gla_prefix_matrix():
    c = GLA_CHUNK
    t = np.arange(c)[:, None]
    u = np.arange(c)[None, :]
    blocks = [(u <= t), (u > t)]
    length = c // 2
    while length >= 1:
        rho = (t // (2 * length)) * 2 * length + length - 1
        right = (t % (2 * length)) >= length
        blocks.append(np.where(right, (u > rho) & (u <= t), (u > t) & (u <= rho)))
        length //= 2
    return np.concatenate(blocks, axis=0).astype(np.float32)


def _gla_pair_masks():
    c = GLA_CHUNK
    t = np.arange(c)[:, None]
    s = np.arange(c)[None, :]
    masks = []
    length = c // 2
    while length >= 1:
        same = (t // (2 * length)) == (s // (2 * length))
        masks.append(same & ((t % (2 * length)) >= length) & ((s % (2 * length)) < length))
        length //= 2
    masks.append(t == s)
    m = np.stack(masks).astype(np.float32)
    return np.tile(m, (1, 1, GLA_HEADS))


def _gla_kernel(q_ref, k_ref, v_ref, r_ref, g_ref, pm_ref, mk_ref, gn_ref, o_ref, st_ref, *, tg):
    @pl.when(pl.program_id(1) == 0)
    def _():
        st_ref[...] = jnp.zeros_like(st_ref)

    c = GLA_CHUNK
    nh = GLA_HEADS
    lane_qk = lax.broadcasted_iota(jnp.int32, (c, GLA_QK), 1)
    head_of_lane = [(lane_qk >= h * GLA_DK) & (lane_qk < (h + 1) * GLA_DK) for h in range(nh)]
    zeros_v = jnp.zeros((c, GLA_DV), BF16)
    ones_c = jnp.ones((c, LANES), BF16)
    gnorm = gn_ref[...]

    def stack_heads(a):
        return jnp.concatenate([jnp.where(head_of_lane[h], a, 0.0) for h in range(nh)],
                               axis=0).astype(BF16)

    def block_diag(blocks):
        rows = []
        for h in range(nh):
            rows.append(jnp.concatenate(
                [blocks[h] if hh == h else zeros_v for hh in range(nh)], axis=1))
        return jnp.concatenate(rows, axis=0)

    def chunk(ci, carry):
        r0 = pl.multiple_of(ci * c, c)
        rows = pl.ds(r0, c)
        q = q_ref[rows, :].astype(F32)
        k = k_ref[rows, :].astype(F32)
        v = v_ref[rows, :]
        g = g_ref[rows, :]
        g_hi = g.astype(BF16)
        g_lo = (g - g_hi.astype(F32)).astype(BF16)
        g2 = jnp.concatenate([g_hi, g_lo], axis=1)
        x2 = _dot(pm_ref[...], g2)
        xs = x2[:, 0:GLA_QK] + x2[:, GLA_QK:2 * GLA_QK]
        b = xs[0:c]
        qe = (q * jnp.exp(b)).astype(BF16)
        kd = (k * jnp.exp(xs[c:2 * c])).astype(BF16)

        att = _dot_nt(q.astype(BF16), stack_heads(k)) * mk_ref[GLA_LEVELS]
        for lv in range(GLA_LEVELS):
            e = jnp.exp(xs[(2 + lv) * c:(3 + lv) * c])
            att = att + _dot_nt((q * e).astype(BF16), stack_heads(k * e)) * mk_ref[lv]

        v_heads = [v[:, h * GLA_DV:(h + 1) * GLA_DV] for h in range(nh)]
        s_heads = [st_ref[h].astype(BF16) for h in range(nh)]
        lhs = jnp.concatenate([att.astype(BF16), qe], axis=1)
        rhs = jnp.concatenate([block_diag(v_heads), block_diag(s_heads)], axis=0)
        o = _dot(lhs, rhs)

        upd = _dot_tn(kd, v)
        bcol2 = _dot_tn(g2, ones_c)
        dcol = jnp.exp(bcol2[0:GLA_QK] + bcol2[GLA_QK:2 * GLA_QK])
        for h in range(nh):
            rs = slice(h * GLA_DK, (h + 1) * GLA_DK)
            st_ref[h] = dcol[rs] * st_ref[h] + upd[rs, h * GLA_DV:(h + 1) * GLA_DV]

        rr = r_ref[rows, :].astype(F32)
        gate = rr * jax.nn.sigmoid(rr)
        for h in range(nh):
            cs = slice(h * GLA_DV, (h + 1) * GLA_DV)
            oh = o[:, cs]
            ms = jnp.mean(oh * oh, axis=-1, keepdims=True)
            o_ref[rows, cs] = (oh * lax.rsqrt(ms + RMS_EPS) * gnorm * gate[:, cs]).astype(BF16)
        return carry

    lax.fori_loop(0, tg // c, chunk, 0)


def _gla_call(gq, gk, gv, gr, la, pm, mk, gn):
    tg = TG_GLA
    nt = SEQ // tg
    row = lambda w: pl.BlockSpec((tg, w), lambda b, i: (b * nt + i, 0))
    return pl.pallas_call(
        functools.partial(_gla_kernel, tg=tg),
        grid=(BATCH, nt),
        in_specs=[row(GLA_QK), row(GLA_QK), row(GLA_VW), row(GLA_VW), row(GLA_QK),
                  _const_spec((GLA_PROWS, GLA_CHUNK)),
                  _const_spec((GLA_LEVELS + 1, GLA_CHUNK, GLA_HEADS * GLA_CHUNK)),
                  _const_spec((1, GLA_DV))],
        out_specs=row(GLA_VW),
        out_shape=jax.ShapeDtypeStruct((TOKENS, GLA_VW), BF16),
        scratch_shapes=[pltpu.VMEM((GLA_HEADS, GLA_DK, GLA_DV), F32)],
        compiler_params=_params("arbitrary", "arbitrary"),
        name="gla_mixer",
    )(gq, gk, gv, gr, la, pm, mk, gn)


def _mla_kernel(qt_ref, k_ref, vt_ref, o_ref, m_ref, acc_ref, st_ref, mx_ref, p_ref, a_ref, *, tile):
    qi = pl.program_id(1)
    qw = tile // 2
    chains = [(h, qh) for h in range(MLA_HEADS) for qh in range(2)]
    n = len(chains)
    ring = MLA_RING
    chunk = MLA_ROW_CHUNK

    def lanes_of(qh):
        return slice(qh * qw, (qh + 1) * qw)

    def scores(j, c, diagonal):
        h, qh = chains[c]
        cols = slice(h * LANES, (h + 1) * LANES)
        k0 = pl.multiple_of(j * tile, tile)
        st = _dot(k_ref[pl.ds(k0, tile), cols], qt_ref[0, cols, lanes_of(qh)])
        if diagonal:
            r = lax.broadcasted_iota(jnp.int32, st.shape, 0)
            col = lax.broadcasted_iota(jnp.int32, st.shape, 1) + qh * qw
            st = jnp.where(r <= col, st, -jnp.inf)
        st_ref[c % ring] = st

    def max_pass(c):
        slot = c % ring
        mx = None
        for r0 in range(0, tile, chunk):
            blk = st_ref[slot, r0:r0 + chunk, :]
            mx = blk if mx is None else jnp.maximum(mx, blk)
            yield
        mx_ref[slot] = jnp.max(mx, axis=0, keepdims=True)

    def exp_pass(c):
        h, qh = chains[c]
        slot = c % ring
        m_prev = m_ref[h, :, lanes_of(qh)]
        m_new = jnp.maximum(m_prev, mx_ref[slot])
        m_ref[h, :, lanes_of(qh)] = m_new
        a_ref[slot] = jnp.exp2(m_prev - m_new)
        for r0 in range(0, tile, chunk):
            p_ref[slot, r0:r0 + chunk, :] = jnp.exp2(st_ref[slot, r0:r0 + chunk, :] - m_new).astype(BF16)
            yield

    def alternate(*gens):
        live = list(gens)
        while live:
            for g in list(live):
                try:
                    next(g)
                except StopIteration:
                    live.remove(g)

    def weighted_values(j, c):
        h, qh = chains[c]
        slot = c % ring
        rows = slice(h * MLA_VT_ROWS, (h + 1) * MLA_VT_ROWS)
        pv = _dot(vt_ref[j, rows, :], p_ref[slot])
        acc_ref[h, :, lanes_of(qh)] = a_ref[slot] * acc_ref[h, :, lanes_of(qh)] + pv

    def pipeline_steps(j, steps, diagonal):
        j_prev = jnp.maximum(j - 1, 0)
        for t in steps:
            if t < n:
                scores(j, t, diagonal)
            gens = []
            if t - 1 < n:
                gens.append(max_pass((t - 1) % n))
            if t - 2 < n:
                gens.append(exp_pass((t - 2) % n))
            alternate(*gens)
            if t - 3 < n:
                weighted_values(j if t >= 3 else j_prev, (t - 3) % n)

    m_ref[...] = jnp.full_like(m_ref, MLA_M_INIT)
    acc_ref[...] = jnp.zeros_like(acc_ref)
    st_ref[(n - 1) % ring] = jnp.full((tile, qw), -jnp.inf, F32)
    st_ref[(n - 2) % ring] = jnp.full((tile, qw), -jnp.inf, F32)
    mx_ref[(n - 2) % ring] = jnp.full((1, qw), -jnp.inf, F32)
    p_ref[(n - 3) % ring] = jnp.zeros((tile, qw), BF16)
    a_ref[(n - 3) % ring] = jnp.ones((1, qw), F32)

    def body(j, carry):
        pipeline_steps(j, range(n), False)
        return carry

    lax.fori_loop(0, qi, body, 0)
    pipeline_steps(qi, range(n), True)
    pipeline_steps(qi, range(n, n + 3), True)

    for h in range(MLA_HEADS):
        num = acc_ref[h, 0:MLA_V, :]
        den = acc_ref[h, MLA_V:MLA_V + 1, :]
        o_ref[:, h * LANES:(h + 1) * LANES] = (num / den).T.astype(BF16)


def _mla_call(qt, k, vt):
    tile = TQ_MLA
    assert tile == TK_MLA == TM_ODD
    nt = SEQ // tile
    qw = tile // 2
    vt_rows = MLA_HEADS * MLA_VT_ROWS
    once = pl.Buffered(1)
    return pl.pallas_call(
        functools.partial(_mla_kernel, tile=tile),
        grid=(BATCH, nt),
        in_specs=[pl.BlockSpec((1, MLA_W, tile), lambda b, i: (b * nt + i, 0, 0)),
                  pl.BlockSpec((SEQ, MLA_W), lambda b, i: (b, 0), pipeline_mode=once),
                  pl.BlockSpec((nt, vt_rows, tile), lambda b, i: (b, 0, 0), pipeline_mode=once)],
        out_specs=pl.BlockSpec((tile, MLA_W), lambda b, i: (b * nt + i, 0)),
        out_shape=jax.ShapeDtypeStruct((TOKENS, MLA_W), BF16),
        scratch_shapes=[pltpu.VMEM((MLA_HEADS, 1, tile), F32),
                        pltpu.VMEM((MLA_HEADS, MLA_VT_ROWS, tile), F32),
                        pltpu.VMEM((MLA_RING, tile, qw), F32),
                        pltpu.VMEM((MLA_RING, 1, qw), F32),
                        pltpu.VMEM((MLA_RING, tile, qw), BF16),
                        pltpu.VMEM((MLA_RING, 1, qw), F32)],
        compiler_params=_params("parallel", "arbitrary"),
        name="mla_attn",
    )(qt, k, vt)


def _rope_tables(dtype):
    pos = jnp.arange(SEQ, dtype=F32)
    inv_freq = ROPE_THETA ** (-jnp.arange(0, MLA_ROPE, 2, dtype=F32) / MLA_ROPE)
    ang = pos[:, None] * inv_freq[None, :]
    cos, sin = jnp.cos(ang).astype(dtype), jnp.sin(ang).astype(dtype)
    zeros = jnp.zeros((SEQ, LANES - MLA_NOPE - MLA_ROPE), dtype)
    ctab = jnp.concatenate([jnp.ones((SEQ, MLA_NOPE), dtype), cos, cos, zeros], axis=1)
    stab = jnp.concatenate([jnp.zeros((SEQ, MLA_NOPE), dtype), -sin, sin, zeros], axis=1)
    return ctab, stab


def _swap_halves(w):
    half = w.shape[-1] // 2
    return jnp.concatenate([w[..., half:], w[..., :half]], axis=-1)


def _place_rope(w):
    rows = w.shape[0]
    return jnp.concatenate([jnp.zeros((rows, MLA_NOPE), w.dtype), w,
                            jnp.zeros((rows, LANES - MLA_NOPE - MLA_ROPE), w.dtype)], axis=1)


def _odd_weights(w_in, w_gate, w_uq, w_ukv):
    o = 0
    gq = w_in[:, o:o + GLA_QK]; o += GLA_QK
    gk = w_in[:, o:o + GLA_QK]; o += GLA_QK
    gv = w_in[:, o:o + GLA_VW]; o += GLA_VW
    g_low = w_in[:, o:o + GLA_GATE_RANK]; o += GLA_GATE_RANK
    gr = w_in[:, o:o + GLA_VW]; o += GLA_VW
    c_q = w_in[:, o:o + MLA_Q_RANK]; o += MLA_Q_RANK
    c_kv = w_in[:, o:o + MLA_KV_RANK]; o += MLA_KV_RANK
    k_r = w_in[:, o:o + MLA_ROPE]
    pad = jnp.zeros((D_MODEL, LANES - GLA_GATE_RANK), w_in.dtype)
    w_ext = jnp.concatenate([gq, gk, gv, gr, c_q, c_kv, g_low, pad,
                             _place_rope(k_r), _place_rope(_swap_halves(k_r))], axis=1)
    wgate = jnp.concatenate([w_gate, jnp.zeros((LANES - GLA_GATE_RANK, GLA_QK), w_gate.dtype)], axis=0)

    per_q = MLA_NOPE + MLA_ROPE
    qa, qb = [], []
    zq = jnp.zeros((MLA_Q_RANK, LANES - per_q), w_uq.dtype)
    for h in range(MLA_HEADS):
        nope = w_uq[:, h * per_q:h * per_q + MLA_NOPE]
        rope = w_uq[:, h * per_q + MLA_NOPE:(h + 1) * per_q]
        qa.append(jnp.concatenate([nope, rope, zq], axis=1))
        qb.append(_place_rope(_swap_halves(rope)))
    wq = jnp.concatenate(qa + qb, axis=1)

    per_kv = MLA_NOPE + MLA_V
    ks, vs = [], []
    zk = jnp.zeros((MLA_KV_RANK, LANES - MLA_NOPE), w_ukv.dtype)
    for h in range(MLA_HEADS):
        ks.append(jnp.concatenate([w_ukv[:, h * per_kv:h * per_kv + MLA_NOPE], zk], axis=1))
        vs.append(w_ukv[:, h * per_kv + MLA_NOPE:(h + 1) * per_kv])
    wkv = jnp.concatenate(ks + vs, axis=1)
    return w_ext.astype(BF16), wgate.astype(BF16), wq.astype(BF16), wkv.astype(BF16)


def kernel(x, ev_w_in, ev_conv_w, ev_sinks, ev_w_out, od_w_in, od_gla_w_gate, od_gla_b_gate,
           od_gla_norm_g, od_mla_q_norm_g, od_mla_w_uq, od_mla_kv_norm_g, od_mla_w_ukv, od_w_out,
           ffn_w_gate, ffn_w_up, ffn_w_down, ln_mix_g, ln_mix_b, ln_ffn_g, ln_ffn_b):
    assert x.shape == (BATCH, SEQ, D_MODEL) and x.dtype == F32
    ctab, stab = _rope_tables(x.dtype)
    pm = jnp.asarray(_gla_prefix_matrix(), BF16)
    mk = jnp.asarray(_gla_pair_masks(), F32)
    half = D_MODEL // 2
    h = x.reshape(TOKENS, D_MODEL)
    for layer in range(DEPTH):
        i = layer // 2
        if layer % 2 == 0:
            y1, y2 = _even_call(h, ev_w_in[i].astype(BF16), ev_conv_w[i], ev_sinks[i])
            w_out = ev_w_out[i]
        else:
            w_ext, wgate, wq, wkv = _odd_weights(od_w_in[i], od_gla_w_gate[i],
                                                 od_mla_w_uq[i], od_mla_w_ukv[i])
            gq, gk, gv, gr, la, mq, mk_, mv = _odd_proj_call(
                h, w_ext, wgate, od_gla_b_gate[i][None, :], od_mla_q_norm_g[i][None, :], wq,
                od_mla_kv_norm_g[i][None, :], wkv, ctab, stab)
            y1 = _gla_call(gq, gk, gv, gr, la, pm, mk, od_gla_norm_g[i][None, :])
            y2 = _mla_call(mq, mk_, mv)
            w_out = od_w_out[i]
        h = _post_call(h, y1, y2, w_out[:half].astype(BF16), w_out[half:].astype(BF16),
                       ln_mix_g[layer][None, :], ln_mix_b[layer][None, :],
                       ffn_w_gate[layer].astype(BF16), ffn_w_up[layer].astype(BF16),
                       ffn_w_down[layer].astype(BF16),
                       ln_ffn_g[layer][None, :], ln_ffn_b[layer][None, :])
    return h.reshape(BATCH, SEQ, D_MODEL)
```

```python
import functools

import numpy as np
import jax
import jax.numpy as jnp
from jax import lax
from jax.experimental import pallas as pl
from jax.experimental.pallas import tpu as pltpu

F32 = jnp.float32
BF16 = jnp.bfloat16

D_MODEL = 1024
BATCH = 2
SEQ = 16384
DEPTH = 4
TOKENS = BATCH * SEQ

CONV_WIDTH = 512
CONV_K = 3
SWA_HEADS = 8
SWA_KV_HEADS = 2
SWA_HEAD_DIM = 64
SWA_BLOCK = 128
GLA_HEADS = 4
GLA_DK = 64
GLA_DV = 128
GLA_GATE_RANK = 16
GLA_GATE_TAU = 16.0
GLA_CHUNK = 64
MLA_HEADS = 4
MLA_Q_RANK = 256
MLA_KV_RANK = 128
MLA_NOPE = 64
MLA_ROPE = 32
MLA_V = 128
ROPE_THETA = 10000.0
D_FF = 2816
ALPHA = (2.0 * DEPTH) ** 0.25
LN_EPS = 1e-5
RMS_EPS = 1e-6
LOG2E = 1.4426950408889634

LANES = 128
SUBLANES = 8
VMEM_LIMIT = 56 * 1024 * 1024

EVEN_IN = 3 * CONV_WIDTH + SWA_HEADS * SWA_HEAD_DIM + 2 * SWA_KV_HEADS * SWA_HEAD_DIM
GLA_QK = GLA_HEADS * GLA_DK
GLA_VW = GLA_HEADS * GLA_DV
MLA_W = MLA_HEADS * LANES
MLA_VT_ROWS = MLA_V + 16
ODD_EXT = 2 * GLA_QK + 2 * GLA_VW + MLA_Q_RANK + MLA_KV_RANK + 3 * LANES
GLA_LEVELS = 6
GLA_PROWS = 4 * GLA_CHUNK
GLA_GROUP = 8

TM_EVEN = 512
TM_ODD = 512
TM_POST = 512
TG_GLA = 512
TQ_MLA = 512
MLA_ROW_CHUNK = 64
MLA_RING = 4
MLA_M_INIT = -3.0e38
TK_MLA = 512


def _params(*sem):
    return pltpu.CompilerParams(dimension_semantics=sem, vmem_limit_bytes=VMEM_LIMIT)


def _const_spec(shape):
    return pl.BlockSpec(shape, lambda *_: (0,) * len(shape), pipeline_mode=pl.Buffered(1))


def _layer_norm(v, g, b):
    mu = jnp.mean(v, axis=-1, keepdims=True)
    d = v - mu
    var = jnp.mean(d * d, axis=-1, keepdims=True)
    return d * lax.rsqrt(var + LN_EPS) * g + b


def _dot(a, b):
    return jnp.dot(a, b, preferred_element_type=F32)


def _dot_nt(a, b):
    return lax.dot_general(a, b, (((1,), (1,)), ((), ())), preferred_element_type=F32)


def _dot_tn(a, b):
    return lax.dot_general(a, b, (((0,), (0,)), ((), ())), preferred_element_type=F32)


def _post_kernel(x_ref, y1_ref, y2_ref, wo1_ref, wo2_ref, g1_ref, b1_ref,
                 wg_ref, wu_ref, wd_ref, g2_ref, b2_ref, o_ref):
    tm = x_ref.shape[0]
    halves = [slice(0, tm // 2), slice(tm // 2, tm)]
    ys = [_dot(y1_ref[r, :], wo1_ref[...]) + _dot(y2_ref[r, :], wo2_ref[...]) for r in halves]
    x1s, hs = [], []
    for r, y in zip(halves, ys):
        x1 = _layer_norm(ALPHA * x_ref[r, :] + y, g1_ref[...], b1_ref[...])
        xb = x1.astype(BF16)
        gate = _dot(xb, wg_ref[...])
        up = _dot(xb, wu_ref[...])
        x1s.append(x1)
        hs.append((gate * jax.nn.sigmoid(gate) * up).astype(BF16))
    fs = [_dot(h, wd_ref[...]) for h in hs]
    for r, x1, f in zip(halves, x1s, fs):
        o_ref[r, :] = _layer_norm(ALPHA * x1 + f, g2_ref[...], b2_ref[...])


def _post_call(x, y1, y2, wo1, wo2, g1, b1, wg, wu, wd, g2, b2):
    tm = TM_POST
    half = D_MODEL // 2
    row = lambda w: pl.BlockSpec((tm, w), lambda i: (i, 0))
    return pl.pallas_call(
        _post_kernel,
        grid=(TOKENS // tm,),
        in_specs=[row(D_MODEL), row(half), row(half),
                  _const_spec((half, D_MODEL)), _const_spec((half, D_MODEL)),
                  _const_spec((1, D_MODEL)), _const_spec((1, D_MODEL)),
                  _const_spec((D_MODEL, D_FF)), _const_spec((D_MODEL, D_FF)),
                  _const_spec((D_FF, D_MODEL)),
                  _const_spec((1, D_MODEL)), _const_spec((1, D_MODEL))],
        out_specs=row(D_MODEL),
        out_shape=jax.ShapeDtypeStruct((TOKENS, D_MODEL), F32),
        compiler_params=_params("parallel"),
        name="post_ffn",
    )(x, y1, y2, wo1, wo2, g1, b1, wg, wu, wd, g2, b2)


def _even_kernel(x_ref, win_ref, cw_ref, sink_ref, ya_ref, yb_ref, zc_ref, kvp_ref, *, tm):
    i = pl.program_id(1)

    @pl.when(i == 0)
    def _():
        zc_ref[...] = jnp.zeros_like(zc_ref)
        kvp_ref[...] = jnp.zeros_like(kvp_ref)

    u = _dot(x_ref[...].astype(BF16), win_ref[...])
    cw = CONV_WIDTH
    b_gate = u[:, 0:cw]
    z = u[:, cw:2 * cw] * u[:, 2 * cw:3 * cw]

    w0 = cw_ref[0:1, :]
    w1 = cw_ref[1:2, :]
    w2 = cw_ref[2:3, :]
    r1 = pltpu.roll(z, 1, 0)
    r2 = pltpu.roll(z, 2, 0)
    c8 = zc_ref[...]
    c1 = pltpu.roll(c8, 1, 0)
    c2 = pltpu.roll(c8, 2, 0)
    rid = lax.broadcasted_iota(jnp.int32, (SUBLANES, cw), 0)
    top1 = jnp.where(rid < 1, c1, r1[0:SUBLANES])
    top2 = jnp.where(rid < 2, c2, r2[0:SUBLANES])
    y_top = top2 * w0 + top1 * w1 + z[0:SUBLANES] * w2
    y_rest = r2[SUBLANES:] * w0 + r1[SUBLANES:] * w1 + z[SUBLANES:] * w2
    ya_ref[0:SUBLANES, :] = (b_gate[0:SUBLANES] * y_top).astype(BF16)
    ya_ref[SUBLANES:, :] = (b_gate[SUBLANES:] * y_rest).astype(BF16)
    zc_ref[...] = z[tm - SUBLANES:tm]

    qoff = 3 * cw
    koff = qoff + SWA_HEADS * SWA_HEAD_DIM
    voff = koff + SWA_KV_HEADS * SWA_HEAD_DIM
    q = u[:, qoff:koff] * (SWA_HEAD_DIM ** -0.5)
    k = u[:, koff:voff].astype(BF16)
    v = u[:, voff:voff + LANES].astype(BF16)
    blk = SWA_BLOCK
    r = lax.broadcasted_iota(jnp.int32, (blk, 2 * blk), 0)
    c = lax.broadcasted_iota(jnp.int32, (blk, 2 * blk), 1)
    own = (c >= blk) & ((c - blk) <= r)
    prev = (c < blk) & (c > r)
    prev_first = (c < blk) & (c > r + jnp.where(i > 0, 0, blk))
    lane = lax.broadcasted_iota(jnp.int32, (blk, LANES), 1)
    low_half = lane < SWA_HEAD_DIM
    group = SWA_HEADS // SWA_KV_HEADS
    mask_inner = own | prev
    mask_first = own | prev_first
    nblk = tm // blk

    def block_scores(j):
        rows = slice(j * blk, (j + 1) * blk)
        if j == 0:
            k_prev = kvp_ref[:, 0:LANES]
            v_prev = kvp_ref[:, LANES:2 * LANES]
        else:
            k_prev = k[(j - 1) * blk:j * blk]
            v_prev = v[(j - 1) * blk:j * blk]
        kk = jnp.concatenate([k_prev, k[rows]], axis=0)
        vv = jnp.concatenate([v_prev, v[rows]], axis=0)
        scores = []
        for g in range(SWA_KV_HEADS):
            in_g = low_half if g == 0 else jnp.logical_not(low_half)
            qs = []
            for i in range(group):
                h = g * group + i
                qp = q[rows, (h // 2) * LANES:(h // 2 + 1) * LANES]
                src = qp if h % 2 == g else pltpu.roll(qp, SWA_HEAD_DIM, 1)
                qs.append(jnp.where(in_g, src, 0.0).astype(BF16))
            scores.append(_dot_nt(jnp.concatenate(qs, axis=0), kk))
        return scores, vv

    def block_finish(j, scores, vv):
        rows = slice(j * blk, (j + 1) * blk)
        mask = mask_first if j == 0 else mask_inner
        ps, dens = [], []
        for g in range(SWA_KV_HEADS):
            ps_g, dens_g = [], []
            for i in range(group):
                sink = sink_ref[g * group + i]
                s = jnp.where(mask, scores[g][i * blk:(i + 1) * blk], -jnp.inf)
                m = jnp.maximum(jnp.max(s, axis=-1, keepdims=True), sink)
                p = jnp.exp(s - m)
                dens_g.append(jnp.sum(p, axis=-1, keepdims=True) + jnp.exp(sink - m))
                ps_g.append(p.astype(BF16))
            ps.append(jnp.concatenate(ps_g, axis=0))
            dens.append(dens_g)
        outs = [_dot(ps[g], vv) for g in range(SWA_KV_HEADS)]
        for g in range(SWA_KV_HEADS):
            for pair in range(group // 2):
                even = outs[g][(2 * pair) * blk:(2 * pair + 1) * blk] / dens[g][2 * pair]
                odd = outs[g][(2 * pair + 1) * blk:(2 * pair + 2) * blk] / dens[g][2 * pair + 1]
                if g == 0:
                    out = jnp.where(low_half, even, pltpu.roll(odd, SWA_HEAD_DIM, 1))
                else:
                    out = jnp.where(low_half, pltpu.roll(even, SWA_HEAD_DIM, 1), odd)
                col = (g * group // 2 + pair) * LANES
                yb_ref[rows, col:col + LANES] = out.astype(BF16)

    pending = None
    for j in range(nblk + 1):
        current = block_scores(j) if j < nblk else None
        if pending is not None:
            block_finish(j - 1, *pending)
        pending = current
    kvp_ref[:, 0:LANES] = k[tm - blk:tm]
    kvp_ref[:, LANES:2 * LANES] = v[tm - blk:tm]


def _even_call(x, w_in, conv_w, sinks):
    tm = TM_EVEN
    nt = SEQ // tm
    row = lambda w: pl.BlockSpec((tm, w), lambda b, i: (b * nt + i, 0))
    half = D_MODEL // 2
    return pl.pallas_call(
        functools.partial(_even_kernel, tm=tm),
        grid=(BATCH, nt),
        in_specs=[row(D_MODEL), _const_spec((D_MODEL, EVEN_IN)), _const_spec((CONV_K, CONV_WIDTH)),
                  pl.BlockSpec(memory_space=pltpu.SMEM)],
        out_specs=[row(half), row(half)],
        out_shape=[jax.ShapeDtypeStruct((TOKENS, half), BF16)] * 2,
        scratch_shapes=[pltpu.VMEM((SUBLANES, CONV_WIDTH), F32),
                        pltpu.VMEM((SWA_BLOCK, 2 * LANES), BF16)],
        compiler_params=_params("arbitrary", "arbitrary"),
        name="even_mixer",
    )(x, w_in, conv_w, sinks)


def _odd_proj_kernel(x_ref, win_ref, wgate_ref, bgate_ref, gq_ref, wq_ref, gkv_ref, wkv_ref,
                     ctab_ref, stab_ref,
                     gq_o, gk_o, gv_o, gr_o, la_o, qt_o, k_o, vt_o):
    u = _dot(x_ref[...].astype(BF16), win_ref[...])
    o = 0
    gq_o[...] = (u[:, o:o + GLA_QK] * (GLA_DK ** -0.5)).astype(BF16); o += GLA_QK
    gk_o[...] = u[:, o:o + GLA_QK].astype(BF16); o += GLA_QK
    gv_o[...] = u[:, o:o + GLA_VW].astype(BF16); o += GLA_VW
    gr_o[...] = u[:, o:o + GLA_VW].astype(BF16); o += GLA_VW
    c_q = u[:, o:o + MLA_Q_RANK]; o += MLA_Q_RANK
    c_kv = u[:, o:o + MLA_KV_RANK]; o += MLA_KV_RANK
    g_low = u[:, o:o + LANES]; o += LANES
    kr = u[:, o:o + LANES]; o += LANES
    kr_sw = u[:, o:o + LANES]

    xg = _dot(g_low.astype(BF16), wgate_ref[...]) + bgate_ref[...]
    log_sig = jnp.minimum(xg, 0.0) - jnp.log(1.0 + jnp.exp(-jnp.abs(xg)))
    la_o[...] = log_sig * (1.0 / GLA_GATE_TAU)

    ctab = ctab_ref[...]
    stab = stab_ref[...]
    cqn = c_q * lax.rsqrt(jnp.mean(c_q * c_q, axis=-1, keepdims=True) + RMS_EPS) * gq_ref[...]
    q2 = _dot(cqn.astype(BF16), wq_ref[...])
    scale = (MLA_NOPE + MLA_ROPE) ** -0.5 * LOG2E
    ckvn = c_kv * lax.rsqrt(jnp.mean(c_kv * c_kv, axis=-1, keepdims=True) + RMS_EPS) * gkv_ref[...]
    kv = _dot(ckvn.astype(BF16), wkv_ref[...])
    k_rope = kr * ctab + kr_sw * stab
    for h in range(MLA_HEADS):
        cols = slice(h * LANES, (h + 1) * LANES)
        qa = q2[:, cols]
        qb = q2[:, MLA_W + h * LANES:MLA_W + (h + 1) * LANES]
        qt_o[0, cols, :] = ((qa * ctab + qb * stab) * scale).T.astype(BF16)
        k_o[:, cols] = (kv[:, cols] + k_rope).astype(BF16)
        vrow = h * MLA_VT_ROWS
        vt_o[0, vrow:vrow + MLA_V, :] = kv[:, MLA_W + h * LANES:MLA_W + (h + 1) * LANES].T.astype(BF16)
        vt_o[0, vrow + MLA_V:vrow + MLA_VT_ROWS, :] = jnp.ones((MLA_VT_ROWS - MLA_V, x_ref.shape[0]), BF16)


def _odd_proj_call(x, w_in, wgate, bgate, gq, wq, gkv, wkv, ctab, stab):
    tm = TM_ODD
    nt = SEQ // tm
    row = lambda w: pl.BlockSpec((tm, w), lambda i: (i, 0))
    tab = pl.BlockSpec((tm, LANES), lambda i: (i % nt, 0))
    tile_t = lambda r: pl.BlockSpec((1, r, tm), lambda i: (i, 0, 0))
    rows_out = [(GLA_QK, BF16), (GLA_QK, BF16), (GLA_VW, BF16), (GLA_VW, BF16), (GLA_QK, F32)]
    row_shape = lambda w, dt: jax.ShapeDtypeStruct((TOKENS, w), dt)
    tile_shape = lambda r: jax.ShapeDtypeStruct((TOKENS // tm, r, tm), BF16)
    vt_rows = MLA_HEADS * MLA_VT_ROWS
    return pl.pallas_call(
        _odd_proj_kernel,
        grid=(TOKENS // tm,),
        in_specs=[row(D_MODEL), _const_spec((D_MODEL, ODD_EXT)),
                  _const_spec((LANES, GLA_QK)), _const_spec((1, GLA_QK)),
                  _const_spec((1, MLA_Q_RANK)), _const_spec((MLA_Q_RANK, 2 * MLA_W)),
                  _const_spec((1, MLA_KV_RANK)), _const_spec((MLA_KV_RANK, 2 * MLA_W)),
                  tab, tab],
        out_specs=[row(w) for w, _ in rows_out] + [tile_t(MLA_W), row(MLA_W), tile_t(vt_rows)],
        out_shape=[row_shape(w, dt) for w, dt in rows_out]
        + [tile_shape(MLA_W), row_shape(MLA_W, BF16), tile_shape(vt_rows)],
        compiler_params=_params("parallel"),
        name="odd_proj",
    )(x, w_in, wgate, bgate, gq, wq, gkv, wkv, ctab, stab)


def _gla_prefix_matrix():
    c = GLA_CHUNK
    t = np.arange(c)[:, None]
    u = np.arange(c)[None, :]
    blocks = [(u <= t)]
    length = SUBLANES // 2
    while length >= 1:
        rho = (t // (2 * length)) * 2 * length + length - 1
        right = (t % (2 * length)) >= length
        blocks.append(np.where(right, (u > rho) & (u <= t), (u > t) & (u <= rho)))
        length //= 2
    return np.concatenate(blocks, axis=0).astype(np.float32)


def _gla_pair_masks():
    c = GLA_CHUNK
    t = np.arange(c)[:, None]
    s = np.arange(c)[None, :]
    masks = []
    length = c // 2
    while length >= 1:
        same = (t // (2 * length)) == (s // (2 * length))
        masks.append(same & ((t % (2 * length)) >= length) & ((s % (2 * length)) < length))
        length //= 2
    masks.append(t == s)
    m = np.stack(masks).astype(np.float32)
    return np.tile(m, (1, 1, GLA_HEADS))


def _gla_kernel(q_ref, k_ref, v_ref, r_ref, g_ref, pm_ref, mk_ref, gn_ref, o_ref, st_ref, *, tg):
    @pl.when(pl.program_id(1) == 0)
    def _():
        st_ref[...] = jnp.zeros_like(st_ref)

    c = GLA_CHUNK
    nh = GLA_HEADS
    lane_qk = lax.broadcasted_iota(jnp.int32, (c, GLA_QK), 1)
    head_of_lane = [(lane_qk >= h * GLA_DK) & (lane_qk < (h + 1) * GLA_DK) for h in range(nh)]
    zeros_v = jnp.zeros((c, GLA_DV), BF16)
    ones_c = jnp.ones((c, LANES), BF16)
    gnorm = gn_ref[...]

    def stack_heads(a):
        return jnp.concatenate([jnp.where(head_of_lane[h], a, 0.0) for h in range(nh)],
                               axis=0).astype(BF16)

    def block_diag(blocks):
        rows = []
        for h in range(nh):
            rows.append(jnp.concatenate(
                [blocks[h] if hh == h else zeros_v for hh in range(nh)], axis=1))
        return jnp.concatenate(rows, axis=0)

    def level_sums(b, length):
        parts = []
        for blk in range(0, c, 2 * length):
            b_rho = b[blk + length - 1:blk + length]
            parts.append(b_rho - b[blk:blk + length])
            parts.append(b[blk + length:blk + 2 * length] - b_rho)
        return jnp.concatenate(parts, axis=0)

    def chunk(ci):
        r0 = pl.multiple_of(ci * c, c)
        rows = pl.ds(r0, c)
        q = q_ref[rows, :].astype(F32)
        k = k_ref[rows, :].astype(F32)
        v = v_ref[rows, :]
        g = g_ref[rows, :]
        g_hi = g.astype(BF16)
        g_lo = (g - g_hi.astype(F32)).astype(BF16)
        g2 = jnp.concatenate([g_hi, g_lo], axis=1)
        x2 = _dot(pm_ref[...], g2)
        xs = x2[:, 0:GLA_QK] + x2[:, GLA_QK:2 * GLA_QK]
        b = xs[0:c]
        yield False
        qe = (q * jnp.exp(b)).astype(BF16)
        kd = (k * jnp.exp(b[c - 1:c] - b)).astype(BF16)
        upd = _dot_tn(kd, v)
        bcol2 = _dot_tn(g2, ones_c)
        yield False

        att = _dot_nt(q.astype(BF16), stack_heads(k)) * mk_ref[GLA_LEVELS]
        yield False
        small = GLA_LEVELS - (GLA_PROWS // c - 1)
        for lv in range(GLA_LEVELS):
            if lv < small:
                x_lv = level_sums(b, c >> (lv + 1))
            else:
                x_lv = xs[(1 + lv - small) * c:(2 + lv - small) * c]
            e = jnp.exp(x_lv)
            att = att + _dot_nt((q * e).astype(BF16), stack_heads(k * e)) * mk_ref[lv]
            yield False

        v_heads = [v[:, h * GLA_DV:(h + 1) * GLA_DV] for h in range(nh)]
        lhs = jnp.concatenate([att.astype(BF16), qe], axis=1)
        vbd = block_diag(v_heads)
        dcol = jnp.exp(bcol2[0:GLA_QK] + bcol2[GLA_QK:2 * GLA_QK])
        rr = r_ref[rows, :].astype(F32)
        gate = rr * jax.nn.sigmoid(rr)
        yield True

        s_heads = [st_ref[h].astype(BF16) for h in range(nh)]
        rhs = jnp.concatenate([vbd, block_diag(s_heads)], axis=0)
        o = _dot(lhs, rhs)
        for h in range(nh):
            rs = slice(h * GLA_DK, (h + 1) * GLA_DK)
            st_ref[h] = dcol[rs] * st_ref[h] + upd[rs, h * GLA_DV:(h + 1) * GLA_DV]
        for h in range(nh):
            cs = slice(h * GLA_DV, (h + 1) * GLA_DV)
            oh = o[:, cs]
            ms = jnp.mean(oh * oh, axis=-1, keepdims=True)
            o_ref[rows, cs] = (oh * lax.rsqrt(ms + RMS_EPS) * gnorm * gate[:, cs]).astype(BF16)

    def chunk_group(gi, carry):
        gens = [chunk(gi * GLA_GROUP + i) for i in range(GLA_GROUP)]
        live = list(gens)
        while live:
            for gen in list(live):
                if next(gen):
                    live.remove(gen)
        for gen in gens:
            for _ in gen:
                pass
        return carry

    lax.fori_loop(0, tg // (c * GLA_GROUP), chunk_group, 0)


def _gla_call(gq, gk, gv, gr, la, pm, mk, gn):
    tg = TG_GLA
    nt = SEQ // tg
    row = lambda w: pl.BlockSpec((tg, w), lambda b, i: (b * nt + i, 0))
    return pl.pallas_call(
        functools.partial(_gla_kernel, tg=tg),
        grid=(BATCH, nt),
        in_specs=[row(GLA_QK), row(GLA_QK), row(GLA_VW), row(GLA_VW), row(GLA_QK),
                  _const_spec((GLA_PROWS, GLA_CHUNK)),
                  _const_spec((GLA_LEVELS + 1, GLA_CHUNK, GLA_HEADS * GLA_CHUNK)),
                  _const_spec((1, GLA_DV))],
        out_specs=row(GLA_VW),
        out_shape=jax.ShapeDtypeStruct((TOKENS, GLA_VW), BF16),
        scratch_shapes=[pltpu.VMEM((GLA_HEADS, GLA_DK, GLA_DV), F32)],
        compiler_params=_params("arbitrary", "arbitrary"),
        name="gla_mixer",
    )(gq, gk, gv, gr, la, pm, mk, gn)


def _mla_kernel(qt_ref, k_ref, vt_ref, o_ref, m_ref, acc_ref, st_ref, mx_ref, p_ref, a_ref, *, tile):
    qi = pl.program_id(1)
    qw = tile // 2
    chains = [(h, qh) for h in range(MLA_HEADS) for qh in range(2)]
    n = len(chains)
    ring = MLA_RING
    chunk = MLA_ROW_CHUNK

    def lanes_of(qh):
        return slice(qh * qw, (qh + 1) * qw)

    def scores(j, c, diagonal):
        h, qh = chains[c]
        cols = slice(h * LANES, (h + 1) * LANES)
        k0 = pl.multiple_of(j * tile, tile)
        st = _dot(k_ref[pl.ds(k0, tile), cols], qt_ref[0, cols, lanes_of(qh)])
        if diagonal:
            r = lax.broadcasted_iota(jnp.int32, st.shape, 0)
            col = lax.broadcasted_iota(jnp.int32, st.shape, 1) + qh * qw
            st = jnp.where(r <= col, st, -jnp.inf)
        st_ref[c % ring] = st

    def max_pass(c):
        slot = c % ring
        mx = None
        for r0 in range(0, tile, chunk):
            blk = st_ref[slot, r0:r0 + chunk, :]
            mx = blk if mx is None else jnp.maximum(mx, blk)
            yield
        mx_ref[slot] = jnp.max(mx, axis=0, keepdims=True)

    def exp_pass(c):
        h, qh = chains[c]
        slot = c % ring
        m_prev = m_ref[h, :, lanes_of(qh)]
        m_new = jnp.maximum(m_prev, mx_ref[slot])
        m_ref[h, :, lanes_of(qh)] = m_new
        a_ref[slot] = jnp.exp2(m_prev - m_new)
        for r0 in range(0, tile, chunk):
            p_ref[slot, r0:r0 + chunk, :] = jnp.exp2(st_ref[slot, r0:r0 + chunk, :] - m_new).astype(BF16)
            yield

    def alternate(*gens):
        live = list(gens)
        while live:
            for g in list(live):
                try:
                    next(g)
                except StopIteration:
                    live.remove(g)

    def weighted_values(j, c):
        h, qh = chains[c]
        slot = c % ring
        rows = slice(h * MLA_VT_ROWS, (h + 1) * MLA_VT_ROWS)
        pv = _dot(vt_ref[j, rows, :], p_ref[slot])
        acc_ref[h, :, lanes_of(qh)] = a_ref[slot] * acc_ref[h, :, lanes_of(qh)] + pv

    def pipeline_steps(j, steps, diagonal):
        j_prev = jnp.maximum(j - 1, 0)
        for t in steps:
            if t < n:
                scores(j, t, diagonal)
            gens = []
            if t - 1 < n:
                gens.append(max_pass((t - 1) % n))
            if t - 2 < n:
                gens.append(exp_pass((t - 2) % n))
            alternate(*gens)
            if t - 3 < n:
                weighted_values(j if t >= 3 else j_prev, (t - 3) % n)

    m_ref[...] = jnp.full_like(m_ref, MLA_M_INIT)
    acc_ref[...] = jnp.zeros_like(acc_ref)
    st_ref[(n - 1) % ring] = jnp.full((tile, qw), -jnp.inf, F32)
    st_ref[(n - 2) % ring] = jnp.full((tile, qw), -jnp.inf, F32)
    mx_ref[(n - 2) % ring] = jnp.full((1, qw), -jnp.inf, F32)
    p_ref[(n - 3) % ring] = jnp.zeros((tile, qw), BF16)
    a_ref[(n - 3) % ring] = jnp.ones((1, qw), F32)

    def body(j, carry):
        pipeline_steps(j, range(n), False)
        return carry

    lax.fori_loop(0, qi, body, 0)
    pipeline_steps(qi, range(n), True)
    pipeline_steps(qi, range(n, n + 3), True)

    for h in range(MLA_HEADS):
        num = acc_ref[h, 0:MLA_V, :]
        den = acc_ref[h, MLA_V:MLA_V + 1, :]
        o_ref[:, h * LANES:(h + 1) * LANES] = (num / den).T.astype(BF16)


def _mla_call(qt, k, vt):
    tile = TQ_MLA
    assert tile == TK_MLA == TM_ODD
    nt = SEQ // tile
    qw = tile // 2
    vt_rows = MLA_HEADS * MLA_VT_ROWS
    once = pl.Buffered(1)
    return pl.pallas_call(
        functools.partial(_mla_kernel, tile=tile),
        grid=(BATCH, nt),
        in_specs=[pl.BlockSpec((1, MLA_W, tile), lambda b, i: (b * nt + i, 0, 0)),
                  pl.BlockSpec((SEQ, MLA_W), lambda b, i: (b, 0), pipeline_mode=once),
                  pl.BlockSpec((nt, vt_rows, tile), lambda b, i: (b, 0, 0), pipeline_mode=once)],
        out_specs=pl.BlockSpec((tile, MLA_W), lambda b, i: (b * nt + i, 0)),
        out_shape=jax.ShapeDtypeStruct((TOKENS, MLA_W), BF16),
        scratch_shapes=[pltpu.VMEM((MLA_HEADS, 1, tile), F32),
                        pltpu.VMEM((MLA_HEADS, MLA_VT_ROWS, tile), F32),
                        pltpu.VMEM((MLA_RING, tile, qw), F32),
                        pltpu.VMEM((MLA_RING, 1, qw), F32),
                        pltpu.VMEM((MLA_RING, tile, qw), BF16),
                        pltpu.VMEM((MLA_RING, 1, qw), F32)],
        compiler_params=_params("parallel", "arbitrary"),
        name="mla_attn",
    )(qt, k, vt)


def _rope_tables(dtype):
    pos = jnp.arange(SEQ, dtype=F32)
    inv_freq = ROPE_THETA ** (-jnp.arange(0, MLA_ROPE, 2, dtype=F32) / MLA_ROPE)
    ang = pos[:, None] * inv_freq[None, :]
    cos, sin = jnp.cos(ang).astype(dtype), jnp.sin(ang).astype(dtype)
    zeros = jnp.zeros((SEQ, LANES - MLA_NOPE - MLA_ROPE), dtype)
    ctab = jnp.concatenate([jnp.ones((SEQ, MLA_NOPE), dtype), cos, cos, zeros], axis=1)
    stab = jnp.concatenate([jnp.zeros((SEQ, MLA_NOPE), dtype), -sin, sin, zeros], axis=1)
    return ctab, stab


def _swap_halves(w):
    half = w.shape[-1] // 2
    return jnp.concatenate([w[..., half:], w[..., :half]], axis=-1)


def _place_rope(w):
    rows = w.shape[0]
    return jnp.concatenate([jnp.zeros((rows, MLA_NOPE), w.dtype), w,
                            jnp.zeros((rows, LANES - MLA_NOPE - MLA_ROPE), w.dtype)], axis=1)


def _odd_weights(w_in, w_gate, w_uq, w_ukv):
    o = 0
    gq = w_in[:, o:o + GLA_QK]; o += GLA_QK
    gk = w_in[:, o:o + GLA_QK]; o += GLA_QK
    gv = w_in[:, o:o + GLA_VW]; o += GLA_VW
    g_low = w_in[:, o:o + GLA_GATE_RANK]; o += GLA_GATE_RANK
    gr = w_in[:, o:o + GLA_VW]; o += GLA_VW
    c_q = w_in[:, o:o + MLA_Q_RANK]; o += MLA_Q_RANK
    c_kv = w_in[:, o:o + MLA_KV_RANK]; o += MLA_KV_RANK
    k_r = w_in[:, o:o + MLA_ROPE]
    pad = jnp.zeros((D_MODEL, LANES - GLA_GATE_RANK), w_in.dtype)
    w_ext = jnp.concatenate([gq, gk, gv, gr, c_q, c_kv, g_low, pad,
                             _place_rope(k_r), _place_rope(_swap_halves(k_r))], axis=1)
    wgate = jnp.concatenate([w_gate, jnp.zeros((LANES - GLA_GATE_RANK, GLA_QK), w_gate.dtype)], axis=0)

    per_q = MLA_NOPE + MLA_ROPE
    qa, qb = [], []
    zq = jnp.zeros((MLA_Q_RANK, LANES - per_q), w_uq.dtype)
    for h in range(MLA_HEADS):
        nope = w_uq[:, h * per_q:h * per_q + MLA_NOPE]
        rope = w_uq[:, h * per_q + MLA_NOPE:(h + 1) * per_q]
        qa.append(jnp.concatenate([nope, rope, zq], axis=1))
        qb.append(_place_rope(_swap_halves(rope)))
    wq = jnp.concatenate(qa + qb, axis=1)

    per_kv = MLA_NOPE + MLA_V
    ks, vs = [], []
    zk = jnp.zeros((MLA_KV_RANK, LANES - MLA_NOPE), w_ukv.dtype)
    for h in range(MLA_HEADS):
        ks.append(jnp.concatenate([w_ukv[:, h * per_kv:h * per_kv + MLA_NOPE], zk], axis=1))
        vs.append(w_ukv[:, h * per_kv + MLA_NOPE:(h + 1) * per_kv])
    wkv = jnp.concatenate(ks + vs, axis=1)
    return w_ext.astype(BF16), wgate.astype(BF16), wq.astype(BF16), wkv.astype(BF16)


def kernel(x, ev_w_in, ev_conv_w, ev_sinks, ev_w_out, od_w_in, od_gla_w_gate, od_gla_b_gate,
           od_gla_norm_g, od_mla_q_norm_g, od_mla_w_uq, od_mla_kv_norm_g, od_mla_w_ukv, od_w_out,
           ffn_w_gate, ffn_w_up, ffn_w_down, ln_mix_g, ln_mix_b, ln_ffn_g, ln_ffn_b):
    assert x.shape == (BATCH, SEQ, D_MODEL) and x.dtype == F32
    ctab, stab = _rope_tables(x.dtype)
    pm = jnp.asarray(_gla_prefix_matrix(), BF16)
    mk = jnp.asarray(_gla_pair_masks(), F32)
    half = D_MODEL // 2
    h = x.reshape(TOKENS, D_MODEL)
    for layer in range(DEPTH):
        i = layer // 2
        if layer % 2 == 0:
            y1, y2 = _even_call(h, ev_w_in[i].astype(BF16), ev_conv_w[i], ev_sinks[i])
            w_out = ev_w_out[i]
        else:
            w_ext, wgate, wq, wkv = _odd_weights(od_w_in[i], od_gla_w_gate[i],
                                                 od_mla_w_uq[i], od_mla_w_ukv[i])
            gq, gk, gv, gr, la, mq, mk_, mv = _odd_proj_call(
                h, w_ext, wgate, od_gla_b_gate[i][None, :], od_mla_q_norm_g[i][None, :], wq,
                od_mla_kv_norm_g[i][None, :], wkv, ctab, stab)
            y1 = _gla_call(gq, gk, gv, gr, la, pm, mk, od_gla_norm_g[i][None, :])
            y2 = _mla_call(mq, mk_, mv)
            w_out = od_w_out[i]
        h = _post_call(h, y1, y2, w_out[:half].astype(BF16), w_out[half:].astype(BF16),
                       ln_mix_g[layer][None, :], ln_mix_b[layer][None, :],
                       ffn_w_gate[layer].astype(BF16), ffn_w_up[layer].astype(BF16),
                       ffn_w_down[layer].astype(BF16),
                       ln_ffn_g[layer][None, :], ln_ffn_b[layer][None, :])
    return h.reshape(BATCH, SEQ, D_MODEL)
```

```python
import functools

import numpy as np
import jax
import jax.numpy as jnp
from jax import lax
from jax.experimental import pallas as pl
from jax.experimental.pallas import tpu as pltpu

F32 = jnp.float32
BF16 = jnp.bfloat16

D_MODEL = 1024
BATCH = 2
SEQ = 16384
DEPTH = 4
TOKENS = BATCH * SEQ

CONV_WIDTH = 512
CONV_K = 3
SWA_HEADS = 8
SWA_KV_HEADS = 2
SWA_HEAD_DIM = 64
SWA_BLOCK = 128
GLA_HEADS = 4
GLA_DK = 64
GLA_DV = 128
GLA_GATE_RANK = 16
GLA_GATE_TAU = 16.0
GLA_CHUNK = 64
MLA_HEADS = 4
MLA_Q_RANK = 256
MLA_KV_RANK = 128
MLA_NOPE = 64
MLA_ROPE = 32
MLA_V = 128
ROPE_THETA = 10000.0
D_FF = 2816
ALPHA = (2.0 * DEPTH) ** 0.25
LN_EPS = 1e-5
RMS_EPS = 1e-6
LOG2E = 1.4426950408889634

LANES = 128
SUBLANES = 8
VMEM_LIMIT = 56 * 1024 * 1024

EVEN_IN = 3 * CONV_WIDTH + SWA_HEADS * SWA_HEAD_DIM + 2 * SWA_KV_HEADS * SWA_HEAD_DIM
GLA_QK = GLA_HEADS * GLA_DK
GLA_VW = GLA_HEADS * GLA_DV
MLA_W = MLA_HEADS * LANES
MLA_VT_ROWS = MLA_V + 16
ODD_EXT = 2 * GLA_QK + 2 * GLA_VW + MLA_Q_RANK + MLA_KV_RANK + 3 * LANES
GLA_LEVELS = 6
GLA_PROWS = 4 * GLA_CHUNK
GLA_GROUP = 8

TM_EVEN = 512
TM_ODD = 512
TM_POST = 512
TG_GLA = 512
TQ_MLA = 512
MLA_ROW_CHUNK = 64
MLA_RING = 4
MLA_M_INIT = -3.0e38
TK_MLA = 512


def _params(*sem):
    return pltpu.CompilerParams(dimension_semantics=sem, vmem_limit_bytes=VMEM_LIMIT)


def _const_spec(shape):
    return pl.BlockSpec(shape, lambda *_: (0,) * len(shape), pipeline_mode=pl.Buffered(1))


def _layer_norm(v, g, b):
    mu = jnp.mean(v, axis=-1, keepdims=True)
    d = v - mu
    var = jnp.mean(d * d, axis=-1, keepdims=True)
    return d * lax.rsqrt(var + LN_EPS) * g + b


def _dot(a, b):
    return jnp.dot(a, b, preferred_element_type=F32)


def _dot_nt(a, b):
    return lax.dot_general(a, b, (((1,), (1,)), ((), ())), preferred_element_type=F32)


def _dot_tn(a, b):
    return lax.dot_general(a, b, (((0,), (0,)), ((), ())), preferred_element_type=F32)


def _post_kernel(x_ref, y1_ref, y2_ref, wo1_ref, wo2_ref, g1_ref, b1_ref,
                 wg_ref, wu_ref, wd_ref, g2_ref, b2_ref, o_ref):
    tm = x_ref.shape[0]
    halves = [slice(0, tm // 2), slice(tm // 2, tm)]
    ys = [_dot(y1_ref[r, :], wo1_ref[...]) + _dot(y2_ref[r, :], wo2_ref[...]) for r in halves]
    x1s, hs = [], []
    for r, y in zip(halves, ys):
        x1 = _layer_norm(ALPHA * x_ref[r, :] + y, g1_ref[...], b1_ref[...])
        xb = x1.astype(BF16)
        gate = _dot(xb, wg_ref[...])
        up = _dot(xb, wu_ref[...])
        x1s.append(x1)
        hs.append((gate * jax.nn.sigmoid(gate) * up).astype(BF16))
    fs = [_dot(h, wd_ref[...]) for h in hs]
    for r, x1, f in zip(halves, x1s, fs):
        o_ref[r, :] = _layer_norm(ALPHA * x1 + f, g2_ref[...], b2_ref[...])


def _post_call(x, y1, y2, wo1, wo2, g1, b1, wg, wu, wd, g2, b2):
    tm = TM_POST
    half = D_MODEL // 2
    row = lambda w: pl.BlockSpec((tm, w), lambda i: (i, 0))
    return pl.pallas_call(
        _post_kernel,
        grid=(TOKENS // tm,),
        in_specs=[row(D_MODEL), row(half), row(half),
                  _const_spec((half, D_MODEL)), _const_spec((half, D_MODEL)),
                  _const_spec((1, D_MODEL)), _const_spec((1, D_MODEL)),
                  _const_spec((D_MODEL, D_FF)), _const_spec((D_MODEL, D_FF)),
                  _const_spec((D_FF, D_MODEL)),
                  _const_spec((1, D_MODEL)), _const_spec((1, D_MODEL))],
        out_specs=row(D_MODEL),
        out_shape=jax.ShapeDtypeStruct((TOKENS, D_MODEL), F32),
        compiler_params=_params("parallel"),
        name="post_ffn",
    )(x, y1, y2, wo1, wo2, g1, b1, wg, wu, wd, g2, b2)


def _even_kernel(x_ref, win_ref, cw_ref, sink_ref, ya_ref, yb_ref, zc_ref, kvp_ref, *, tm):
    i = pl.program_id(1)

    @pl.when(i == 0)
    def _():
        zc_ref[...] = jnp.zeros_like(zc_ref)
        kvp_ref[...] = jnp.zeros_like(kvp_ref)

    cw = CONV_WIDTH
    blk = SWA_BLOCK
    half_rows = tm // 2
    blocks_per_half = half_rows // blk
    qoff = 3 * cw
    koff = qoff + SWA_HEADS * SWA_HEAD_DIM
    voff = koff + SWA_KV_HEADS * SWA_HEAD_DIM
    group = SWA_HEADS // SWA_KV_HEADS

    def project(hf):
        r0 = hf * half_rows
        u = _dot(x_ref[r0:r0 + half_rows, :].astype(BF16), win_ref[...])
        return dict(b_gate=u[:, 0:cw], z=u[:, cw:2 * cw] * u[:, 2 * cw:3 * cw],
                    q=u[:, qoff:koff] * (SWA_HEAD_DIM ** -0.5),
                    k=u[:, koff:voff].astype(BF16), v=u[:, voff:voff + LANES].astype(BF16))

    w0 = cw_ref[0:1, :]
    w1 = cw_ref[1:2, :]
    w2 = cw_ref[2:3, :]
    rid = lax.broadcasted_iota(jnp.int32, (SUBLANES, cw), 0)

    def conv_half(hf, part, c8):
        r0 = hf * half_rows
        z, b_gate = part["z"], part["b_gate"]
        r1 = pltpu.roll(z, 1, 0)
        r2 = pltpu.roll(z, 2, 0)
        top1 = jnp.where(rid < 1, pltpu.roll(c8, 1, 0), r1[0:SUBLANES])
        top2 = jnp.where(rid < 2, pltpu.roll(c8, 2, 0), r2[0:SUBLANES])
        y_top = top2 * w0 + top1 * w1 + z[0:SUBLANES] * w2
        y_rest = r2[SUBLANES:] * w0 + r1[SUBLANES:] * w1 + z[SUBLANES:] * w2
        ya_ref[r0:r0 + SUBLANES, :] = (b_gate[0:SUBLANES] * y_top).astype(BF16)
        ya_ref[r0 + SUBLANES:r0 + half_rows, :] = (b_gate[SUBLANES:] * y_rest).astype(BF16)
        return z[half_rows - SUBLANES:half_rows]

    r = lax.broadcasted_iota(jnp.int32, (blk, 2 * blk), 0)
    c = lax.broadcasted_iota(jnp.int32, (blk, 2 * blk), 1)
    own = (c >= blk) & ((c - blk) <= r)
    prev = (c < blk) & (c > r)
    prev_first = (c < blk) & (c > r + jnp.where(i > 0, 0, blk))
    lane = lax.broadcasted_iota(jnp.int32, (blk, LANES), 1)
    low_half = lane < SWA_HEAD_DIM
    mask_inner = own | prev
    mask_first = own | prev_first

    def block_scores(part, jl, k_prev, v_prev):
        rows = slice(jl * blk, (jl + 1) * blk)
        q, k, v = part["q"], part["k"], part["v"]
        kk = jnp.concatenate([k_prev, k[rows]], axis=0)
        vv = jnp.concatenate([v_prev, v[rows]], axis=0)
        scores = []
        for g in range(SWA_KV_HEADS):
            in_g = low_half if g == 0 else jnp.logical_not(low_half)
            qs = []
            for hi in range(group):
                h = g * group + hi
                qp = q[rows, (h // 2) * LANES:(h // 2 + 1) * LANES]
                src = qp if h % 2 == g else pltpu.roll(qp, SWA_HEAD_DIM, 1)
                qs.append(jnp.where(in_g, src, 0.0).astype(BF16))
            scores.append(_dot_nt(jnp.concatenate(qs, axis=0), kk))
        return scores, vv, k[rows], v[rows]

    def block_finish(j, scores, vv):
        rows = slice(j * blk, (j + 1) * blk)
        mask = mask_first if j == 0 else mask_inner
        ps, dens = [], []
        for g in range(SWA_KV_HEADS):
            ps_g, dens_g = [], []
            for hi in range(group):
                sink = sink_ref[g * group + hi]
                s = jnp.where(mask, scores[g][hi * blk:(hi + 1) * blk], -jnp.inf)
                m = jnp.maximum(jnp.max(s, axis=-1, keepdims=True), sink)
                p = jnp.exp(s - m)
                dens_g.append(jnp.sum(p, axis=-1, keepdims=True) + jnp.exp(sink - m))
                ps_g.append(p.astype(BF16))
            ps.append(jnp.concatenate(ps_g, axis=0))
            dens.append(dens_g)
        outs = [_dot(ps[g], vv) for g in range(SWA_KV_HEADS)]
        for g in range(SWA_KV_HEADS):
            for pair in range(group // 2):
                even = outs[g][(2 * pair) * blk:(2 * pair + 1) * blk] / dens[g][2 * pair]
                odd = outs[g][(2 * pair + 1) * blk:(2 * pair + 2) * blk] / dens[g][2 * pair + 1]
                if g == 0:
                    out = jnp.where(low_half, even, pltpu.roll(odd, SWA_HEAD_DIM, 1))
                else:
                    out = jnp.where(low_half, pltpu.roll(even, SWA_HEAD_DIM, 1), odd)
                col = (g * group // 2 + pair) * LANES
                yb_ref[rows, col:col + LANES] = out.astype(BF16)

    k_prev = kvp_ref[:, 0:LANES]
    v_prev = kvp_ref[:, LANES:2 * LANES]
    c8 = zc_ref[...]
    waiting = None
    for hf in range(3):
        if hf < 2:
            part = project(hf)
            scored = []
            for jl in range(blocks_per_half):
                scores, vv, k_prev, v_prev = block_scores(part, jl, k_prev, v_prev)
                scored.append((hf * blocks_per_half + jl, scores, vv))
        if waiting is not None:
            w_hf, w_part, w_scored = waiting
            c8 = conv_half(w_hf, w_part, c8)
            for j, scores, vv in w_scored:
                block_finish(j, scores, vv)
        waiting = (hf, part, scored) if hf < 2 else None
    zc_ref[...] = c8
    kvp_ref[:, 0:LANES] = k_prev
    kvp_ref[:, LANES:2 * LANES] = v_prev


def _even_call(x, w_in, conv_w, sinks):
    tm = TM_EVEN
    nt = SEQ // tm
    row = lambda w: pl.BlockSpec((tm, w), lambda b, i: (b * nt + i, 0))
    half = D_MODEL // 2
    return pl.pallas_call(
        functools.partial(_even_kernel, tm=tm),
        grid=(BATCH, nt),
        in_specs=[row(D_MODEL), _const_spec((D_MODEL, EVEN_IN)), _const_spec((CONV_K, CONV_WIDTH)),
                  pl.BlockSpec(memory_space=pltpu.SMEM)],
        out_specs=[row(half), row(half)],
        out_shape=[jax.ShapeDtypeStruct((TOKENS, half), BF16)] * 2,
        scratch_shapes=[pltpu.VMEM((SUBLANES, CONV_WIDTH), F32),
                        pltpu.VMEM((SWA_BLOCK, 2 * LANES), BF16)],
        compiler_params=_params("arbitrary", "arbitrary"),
        name="even_mixer",
    )(x, w_in, conv_w, sinks)


def _odd_proj_kernel(x_ref, win_ref, wgate_ref, bgate_ref, gq_ref, wq_ref, gkv_ref, wkv_ref,
                     ctab_ref, stab_ref,
                     gq_o, gk_o, gv_o, gr_o, la_o, qt_o, k_o, vt_o):
    u = _dot(x_ref[...].astype(BF16), win_ref[...])
    o = 0
    gq_o[...] = (u[:, o:o + GLA_QK] * (GLA_DK ** -0.5)).astype(BF16); o += GLA_QK
    gk_o[...] = u[:, o:o + GLA_QK].astype(BF16); o += GLA_QK
    gv_o[...] = u[:, o:o + GLA_VW].astype(BF16); o += GLA_VW
    gr_o[...] = u[:, o:o + GLA_VW].astype(BF16); o += GLA_VW
    c_q = u[:, o:o + MLA_Q_RANK]; o += MLA_Q_RANK
    c_kv = u[:, o:o + MLA_KV_RANK]; o += MLA_KV_RANK
    g_low = u[:, o:o + LANES]; o += LANES
    kr = u[:, o:o + LANES]; o += LANES
    kr_sw = u[:, o:o + LANES]

    xg = _dot(g_low.astype(BF16), wgate_ref[...]) + bgate_ref[...]
    log_sig = jnp.minimum(xg, 0.0) - jnp.log(1.0 + jnp.exp(-jnp.abs(xg)))
    la_o[...] = log_sig * (1.0 / GLA_GATE_TAU)

    ctab = ctab_ref[...]
    stab = stab_ref[...]
    cqn = c_q * lax.rsqrt(jnp.mean(c_q * c_q, axis=-1, keepdims=True) + RMS_EPS) * gq_ref[...]
    q2 = _dot(cqn.astype(BF16), wq_ref[...])
    scale = (MLA_NOPE + MLA_ROPE) ** -0.5 * LOG2E
    ckvn = c_kv * lax.rsqrt(jnp.mean(c_kv * c_kv, axis=-1, keepdims=True) + RMS_EPS) * gkv_ref[...]
    kv = _dot(ckvn.astype(BF16), wkv_ref[...])
    k_rope = kr * ctab + kr_sw * stab
    for h in range(MLA_HEADS):
        cols = slice(h * LANES, (h + 1) * LANES)
        qa = q2[:, cols]
        qb = q2[:, MLA_W + h * LANES:MLA_W + (h + 1) * LANES]
        qt_o[0, cols, :] = ((qa * ctab + qb * stab) * scale).T.astype(BF16)
        k_o[:, cols] = (kv[:, cols] + k_rope).astype(BF16)
        vrow = h * MLA_VT_ROWS
        vt_o[0, vrow:vrow + MLA_V, :] = kv[:, MLA_W + h * LANES:MLA_W + (h + 1) * LANES].T.astype(BF16)
        vt_o[0, vrow + MLA_V:vrow + MLA_VT_ROWS, :] = jnp.ones((MLA_VT_ROWS - MLA_V, x_ref.shape[0]), BF16)


def _odd_proj_call(x, w_in, wgate, bgate, gq, wq, gkv, wkv, ctab, stab):
    tm = TM_ODD
    nt = SEQ // tm
    row = lambda w: pl.BlockSpec((tm, w), lambda i: (i, 0))
    tab = pl.BlockSpec((tm, LANES), lambda i: (i % nt, 0))
    tile_t = lambda r: pl.BlockSpec((1, r, tm), lambda i: (i, 0, 0))
    rows_out = [(GLA_QK, BF16), (GLA_QK, BF16), (GLA_VW, BF16), (GLA_VW, BF16), (GLA_QK, F32)]
    row_shape = lambda w, dt: jax.ShapeDtypeStruct((TOKENS, w), dt)
    tile_shape = lambda r: jax.ShapeDtypeStruct((TOKENS // tm, r, tm), BF16)
    vt_rows = MLA_HEADS * MLA_VT_ROWS
    return pl.pallas_call(
        _odd_proj_kernel,
        grid=(TOKENS // tm,),
        in_specs=[row(D_MODEL), _const_spec((D_MODEL, ODD_EXT)),
                  _const_spec((LANES, GLA_QK)), _const_spec((1, GLA_QK)),
                  _const_spec((1, MLA_Q_RANK)), _const_spec((MLA_Q_RANK, 2 * MLA_W)),
                  _const_spec((1, MLA_KV_RANK)), _const_spec((MLA_KV_RANK, 2 * MLA_W)),
                  tab, tab],
        out_specs=[row(w) for w, _ in rows_out] + [tile_t(MLA_W), row(MLA_W), tile_t(vt_rows)],
        out_shape=[row_shape(w, dt) for w, dt in rows_out]
        + [tile_shape(MLA_W), row_shape(MLA_W, BF16), tile_shape(vt_rows)],
        compiler_params=_params("parallel"),
        name="odd_proj",
    )(x, w_in, wgate, bgate, gq, wq, gkv, wkv, ctab, stab)


def _gla_prefix_matrix():
    c = GLA_CHUNK
    t = np.arange(c)[:, None]
    u = np.arange(c)[None, :]
    blocks = [(u <= t)]
    length = SUBLANES // 2
    while length >= 1:
        rho = (t // (2 * length)) * 2 * length + length - 1
        right = (t % (2 * length)) >= length
        blocks.append(np.where(right, (u > rho) & (u <= t), (u > t) & (u <= rho)))
        length //= 2
    return np.concatenate(blocks, axis=0).astype(np.float32)


def _gla_pair_masks():
    c = GLA_CHUNK
    t = np.arange(c)[:, None]
    s = np.arange(c)[None, :]
    masks = []
    length = c // 2
    while length >= 1:
        same = (t // (2 * length)) == (s // (2 * length))
        masks.append(same & ((t % (2 * length)) >= length) & ((s % (2 * length)) < length))
        length //= 2
    masks.append(t == s)
    m = np.stack(masks).astype(np.float32)
    return np.tile(m, (1, 1, GLA_HEADS))


def _gla_kernel(q_ref, k_ref, v_ref, r_ref, g_ref, pm_ref, mk_ref, gn_ref, o_ref, st_ref, *, tg):
    @pl.when(pl.program_id(1) == 0)
    def _():
        st_ref[...] = jnp.zeros_like(st_ref)

    c = GLA_CHUNK
    nh = GLA_HEADS
    lane_qk = lax.broadcasted_iota(jnp.int32, (c, GLA_QK), 1)
    head_of_lane = [(lane_qk >= h * GLA_DK) & (lane_qk < (h + 1) * GLA_DK) for h in range(nh)]
    zeros_v = jnp.zeros((c, GLA_DV), BF16)
    ones_c = jnp.ones((c, LANES), BF16)
    gnorm = gn_ref[...]

    def stack_heads(a):
        return jnp.concatenate([jnp.where(head_of_lane[h], a, 0.0) for h in range(nh)],
                               axis=0).astype(BF16)

    def block_diag(blocks):
        rows = []
        for h in range(nh):
            rows.append(jnp.concatenate(
                [blocks[h] if hh == h else zeros_v for hh in range(nh)], axis=1))
        return jnp.concatenate(rows, axis=0)

    def level_sums(b, length):
        parts = []
        for blk in range(0, c, 2 * length):
            b_rho = b[blk + length - 1:blk + length]
            parts.append(b_rho - b[blk:blk + length])
            parts.append(b[blk + length:blk + 2 * length] - b_rho)
        return jnp.concatenate(parts, axis=0)

    def chunk(ci):
        r0 = pl.multiple_of(ci * c, c)
        rows = pl.ds(r0, c)
        q = q_ref[rows, :].astype(F32)
        k = k_ref[rows, :].astype(F32)
        v = v_ref[rows, :]
        g = g_ref[rows, :]
        g_hi = g.astype(BF16)
        g_lo = (g - g_hi.astype(F32)).astype(BF16)
        g2 = jnp.concatenate([g_hi, g_lo], axis=1)
        x2 = _dot(pm_ref[...], g2)
        xs = x2[:, 0:GLA_QK] + x2[:, GLA_QK:2 * GLA_QK]
        b = xs[0:c]
        yield False
        qe = (q * jnp.exp(b)).astype(BF16)
        kd = (k * jnp.exp(b[c - 1:c] - b)).astype(BF16)
        upd = _dot_tn(kd, v)
        bcol2 = _dot_tn(g2, ones_c)
        yield False

        att = _dot_nt(q.astype(BF16), stack_heads(k)) * mk_ref[GLA_LEVELS]
        yield False
        small = GLA_LEVELS - (GLA_PROWS // c - 1)
        for lv in range(GLA_LEVELS):
            if lv < small:
                x_lv = level_sums(b, c >> (lv + 1))
            else:
                x_lv = xs[(1 + lv - small) * c:(2 + lv - small) * c]
            e = jnp.exp(x_lv)
            att = att + _dot_nt((q * e).astype(BF16), stack_heads(k * e)) * mk_ref[lv]
            yield False

        v_heads = [v[:, h * GLA_DV:(h + 1) * GLA_DV] for h in range(nh)]
        lhs = jnp.concatenate([att.astype(BF16), qe], axis=1)
        vbd = block_diag(v_heads)
        dcol = jnp.exp(bcol2[0:GLA_QK] + bcol2[GLA_QK:2 * GLA_QK])
        rr = r_ref[rows, :].astype(F32)
        gate = rr * jax.nn.sigmoid(rr)
        yield True

        s_heads = [st_ref[h].astype(BF16) for h in range(nh)]
        rhs = jnp.concatenate([vbd, block_diag(s_heads)], axis=0)
        o = _dot(lhs, rhs)
        for h in range(nh):
            rs = slice(h * GLA_DK, (h + 1) * GLA_DK)
            st_ref[h] = dcol[rs] * st_ref[h] + upd[rs, h * GLA_DV:(h + 1) * GLA_DV]
        for h in range(nh):
            cs = slice(h * GLA_DV, (h + 1) * GLA_DV)
            oh = o[:, cs]
            ms = jnp.mean(oh * oh, axis=-1, keepdims=True)
            o_ref[rows, cs] = (oh * lax.rsqrt(ms + RMS_EPS) * gnorm * gate[:, cs]).astype(BF16)

    def chunk_group(gi, carry):
        gens = [chunk(gi * GLA_GROUP + i) for i in range(GLA_GROUP)]
        live = list(gens)
        while live:
            for gen in list(live):
                if next(gen):
                    live.remove(gen)
        for gen in gens:
            for _ in gen:
                pass
        return carry

    lax.fori_loop(0, tg // (c * GLA_GROUP), chunk_group, 0)


def _gla_call(gq, gk, gv, gr, la, pm, mk, gn):
    tg = TG_GLA
    nt = SEQ // tg
    row = lambda w: pl.BlockSpec((tg, w), lambda b, i: (b * nt + i, 0))
    return pl.pallas_call(
        functools.partial(_gla_kernel, tg=tg),
        grid=(BATCH, nt),
        in_specs=[row(GLA_QK), row(GLA_QK), row(GLA_VW), row(GLA_VW), row(GLA_QK),
                  _const_spec((GLA_PROWS, GLA_CHUNK)),
                  _const_spec((GLA_LEVELS + 1, GLA_CHUNK, GLA_HEADS * GLA_CHUNK)),
                  _const_spec((1, GLA_DV))],
        out_specs=row(GLA_VW),
        out_shape=jax.ShapeDtypeStruct((TOKENS, GLA_VW), BF16),
        scratch_shapes=[pltpu.VMEM((GLA_HEADS, GLA_DK, GLA_DV), F32)],
        compiler_params=_params("arbitrary", "arbitrary"),
        name="gla_mixer",
    )(gq, gk, gv, gr, la, pm, mk, gn)


def _mla_kernel(qt_ref, k_ref, vt_ref, o_ref, m_ref, acc_ref, st_ref, mx_ref, p_ref, a_ref, *, tile):
    qi = pl.program_id(1)
    qw = tile // 2
    chains = [(h, qh) for h in range(MLA_HEADS) for qh in range(2)]
    n = len(chains)
    ring = MLA_RING
    chunk = MLA_ROW_CHUNK

    def lanes_of(qh):
        return slice(qh * qw, (qh + 1) * qw)

    def scores(j, c, diagonal):
        h, qh = chains[c]
        cols = slice(h * LANES, (h + 1) * LANES)
        k0 = pl.multiple_of(j * tile, tile)
        st = _dot(k_ref[pl.ds(k0, tile), cols], qt_ref[0, cols, lanes_of(qh)])
        if diagonal:
            r = lax.broadcasted_iota(jnp.int32, st.shape, 0)
            col = lax.broadcasted_iota(jnp.int32, st.shape, 1) + qh * qw
            st = jnp.where(r <= col, st, -jnp.inf)
        st_ref[c % ring] = st

    def max_pass(c):
        slot = c % ring
        mx = None
        for r0 in range(0, tile, chunk):
            blk = st_ref[slot, r0:r0 + chunk, :]
            mx = blk if mx is None else jnp.maximum(mx, blk)
            yield
        mx_ref[slot] = jnp.max(mx, axis=0, keepdims=True)

    def exp_pass(c):
        h, qh = chains[c]
        slot = c % ring
        m_prev = m_ref[h, :, lanes_of(qh)]
        m_new = jnp.maximum(m_prev, mx_ref[slot])
        m_ref[h, :, lanes_of(qh)] = m_new
        a_ref[slot] = jnp.exp2(m_prev - m_new)
        for r0 in range(0, tile, chunk):
            p_ref[slot, r0:r0 + chunk, :] = jnp.exp2(st_ref[slot, r0:r0 + chunk, :] - m_new).astype(BF16)
            yield

    def alternate(*gens):
        live = list(gens)
        while live:
            for g in list(live):
                try:
                    next(g)
                except StopIteration:
                    live.remove(g)

    def weighted_values(j, c):
        h, qh = chains[c]
        slot = c % ring
        rows = slice(h * MLA_VT_ROWS, (h + 1) * MLA_VT_ROWS)
        pv = _dot(vt_ref[j, rows, :], p_ref[slot])
        acc_ref[h, :, lanes_of(qh)] = a_ref[slot] * acc_ref[h, :, lanes_of(qh)] + pv

    def pipeline_steps(j, steps, diagonal):
        j_prev = jnp.maximum(j - 1, 0)
        for t in steps:
            if t < n:
                scores(j, t, diagonal)
            gens = []
            if t - 1 < n:
                gens.append(max_pass((t - 1) % n))
            if t - 2 < n:
                gens.append(exp_pass((t - 2) % n))
            alternate(*gens)
            if t - 3 < n:
                weighted_values(j if t >= 3 else j_prev, (t - 3) % n)

    m_ref[...] = jnp.full_like(m_ref, MLA_M_INIT)
    acc_ref[...] = jnp.zeros_like(acc_ref)
    st_ref[(n - 1) % ring] = jnp.full((tile, qw), -jnp.inf, F32)
    st_ref[(n - 2) % ring] = jnp.full((tile, qw), -jnp.inf, F32)
    mx_ref[(n - 2) % ring] = jnp.full((1, qw), -jnp.inf, F32)
    p_ref[(n - 3) % ring] = jnp.zeros((tile, qw), BF16)
    a_ref[(n - 3) % ring] = jnp.ones((1, qw), F32)

    def body(i, carry):
        pipeline_steps(2 * i, range(n), False)
        pipeline_steps(2 * i + 1, range(n), False)
        return carry

    lax.fori_loop(0, qi // 2, body, 0)

    @pl.when(qi % 2 == 1)
    def _():
        pipeline_steps(qi - 1, range(n), False)

    pipeline_steps(qi, range(n), True)
    pipeline_steps(qi, range(n, n + 3), True)

    for h in range(MLA_HEADS):
        num = acc_ref[h, 0:MLA_V, :]
        den = acc_ref[h, MLA_V:MLA_V + 1, :]
        o_ref[:, h * LANES:(h + 1) * LANES] = (num / den).T.astype(BF16)


def _mla_call(qt, k, vt):
    tile = TQ_MLA
    assert tile == TK_MLA == TM_ODD
    nt = SEQ // tile
    qw = tile // 2
    vt_rows = MLA_HEADS * MLA_VT_ROWS
    once = pl.Buffered(1)
    return pl.pallas_call(
        functools.partial(_mla_kernel, tile=tile),
        grid=(BATCH, nt),
        in_specs=[pl.BlockSpec((1, MLA_W, tile), lambda b, i: (b * nt + i, 0, 0)),
                  pl.BlockSpec((SEQ, MLA_W), lambda b, i: (b, 0), pipeline_mode=once),
                  pl.BlockSpec((nt, vt_rows, tile), lambda b, i: (b, 0, 0), pipeline_mode=once)],
        out_specs=pl.BlockSpec((tile, MLA_W), lambda b, i: (b * nt + i, 0)),
        out_shape=jax.ShapeDtypeStruct((TOKENS, MLA_W), BF16),
        scratch_shapes=[pltpu.VMEM((MLA_HEADS, 1, tile), F32),
                        pltpu.VMEM((MLA_HEADS, MLA_VT_ROWS, tile), F32),
                        pltpu.VMEM((MLA_RING, tile, qw), F32),
                        pltpu.VMEM((MLA_RING, 1, qw), F32),
                        pltpu.VMEM((MLA_RING, tile, qw), BF16),
                        pltpu.VMEM((MLA_RING, 1, qw), F32)],
        compiler_params=_params("parallel", "arbitrary"),
        name="mla_attn",
    )(qt, k, vt)


def _rope_tables(dtype):
    pos = jnp.arange(SEQ, dtype=F32)
    inv_freq = ROPE_THETA ** (-jnp.arange(0, MLA_ROPE, 2, dtype=F32) / MLA_ROPE)
    ang = pos[:, None] * inv_freq[None, :]
    cos, sin = jnp.cos(ang).astype(dtype), jnp.sin(ang).astype(dtype)
    zeros = jnp.zeros((SEQ, LANES - MLA_NOPE - MLA_ROPE), dtype)
    ctab = jnp.concatenate([jnp.ones((SEQ, MLA_NOPE), dtype), cos, cos, zeros], axis=1)
    stab = jnp.concatenate([jnp.zeros((SEQ, MLA_NOPE), dtype), -sin, sin, zeros], axis=1)
    return ctab, stab


def _swap_halves(w):
    half = w.shape[-1] // 2
    return jnp.concatenate([w[..., half:], w[..., :half]], axis=-1)


def _place_rope(w):
    rows = w.shape[0]
    return jnp.concatenate([jnp.zeros((rows, MLA_NOPE), w.dtype), w,
                            jnp.zeros((rows, LANES - MLA_NOPE - MLA_ROPE), w.dtype)], axis=1)


def _odd_weights(w_in, w_gate, w_uq, w_ukv):
    o = 0
    gq = w_in[:, o:o + GLA_QK]; o += GLA_QK
    gk = w_in[:, o:o + GLA_QK]; o += GLA_QK
    gv = w_in[:, o:o + GLA_VW]; o += GLA_VW
    g_low = w_in[:, o:o + GLA_GATE_RANK]; o += GLA_GATE_RANK
    gr = w_in[:, o:o + GLA_VW]; o += GLA_VW
    c_q = w_in[:, o:o + MLA_Q_RANK]; o += MLA_Q_RANK
    c_kv = w_in[:, o:o + MLA_KV_RANK]; o += MLA_KV_RANK
    k_r = w_in[:, o:o + MLA_ROPE]
    pad = jnp.zeros((D_MODEL, LANES - GLA_GATE_RANK), w_in.dtype)
    w_ext = jnp.concatenate([gq, gk, gv, gr, c_q, c_kv, g_low, pad,
                             _place_rope(k_r), _place_rope(_swap_halves(k_r))], axis=1)
    wgate = jnp.concatenate([w_gate, jnp.zeros((LANES - GLA_GATE_RANK, GLA_QK), w_gate.dtype)], axis=0)

    per_q = MLA_NOPE + MLA_ROPE
    qa, qb = [], []
    zq = jnp.zeros((MLA_Q_RANK, LANES - per_q), w_uq.dtype)
    for h in range(MLA_HEADS):
        nope = w_uq[:, h * per_q:h * per_q + MLA_NOPE]
        rope = w_uq[:, h * per_q + MLA_NOPE:(h + 1) * per_q]
        qa.append(jnp.concatenate([nope, rope, zq], axis=1))
        qb.append(_place_rope(_swap_halves(rope)))
    wq = jnp.concatenate(qa + qb, axis=1)

    per_kv = MLA_NOPE + MLA_V
    ks, vs = [], []
    zk = jnp.zeros((MLA_KV_RANK, LANES - MLA_NOPE), w_ukv.dtype)
    for h in range(MLA_HEADS):
        ks.append(jnp.concatenate([w_ukv[:, h * per_kv:h * per_kv + MLA_NOPE], zk], axis=1))
        vs.append(w_ukv[:, h * per_kv + MLA_NOPE:(h + 1) * per_kv])
    wkv = jnp.concatenate(ks + vs, axis=1)
    return w_ext.astype(BF16), wgate.astype(BF16), wq.astype(BF16), wkv.astype(BF16)


def kernel(x, ev_w_in, ev_conv_w, ev_sinks, ev_w_out, od_w_in, od_gla_w_gate, od_gla_b_gate,
           od_gla_norm_g, od_mla_q_norm_g, od_mla_w_uq, od_mla_kv_norm_g, od_mla_w_ukv, od_w_out,
           ffn_w_gate, ffn_w_up, ffn_w_down, ln_mix_g, ln_mix_b, ln_ffn_g, ln_ffn_b):
    assert x.shape == (BATCH, SEQ, D_MODEL) and x.dtype == F32
    ctab, stab = _rope_tables(x.dtype)
    pm = jnp.asarray(_gla_prefix_matrix(), BF16)
    mk = jnp.asarray(_gla_pair_masks(), F32)
    half = D_MODEL // 2
    h = x.reshape(TOKENS, D_MODEL)
    for layer in range(DEPTH):
        i = layer // 2
        if layer % 2 == 0:
            y1, y2 = _even_call(h, ev_w_in[i].astype(BF16), ev_conv_w[i], ev_sinks[i])
            w_out = ev_w_out[i]
        else:
            w_ext, wgate, wq, wkv = _odd_weights(od_w_in[i], od_gla_w_gate[i],
                                                 od_mla_w_uq[i], od_mla_w_ukv[i])
            gq, gk, gv, gr, la, mq, mk_, mv = _odd_proj_call(
                h, w_ext, wgate, od_gla_b_gate[i][None, :], od_mla_q_norm_g[i][None, :], wq,
                od_mla_kv_norm_g[i][None, :], wkv, ctab, stab)
            y1 = _gla_call(gq, gk, gv, gr, la, pm, mk, od_gla_norm_g[i][None, :])
            y2 = _mla_call(mq, mk_, mv)
            w_out = od_w_out[i]
        h = _post_call(h, y1, y2, w_out[:half].astype(BF16), w_out[half:].astype(BF16),
                       ln_mix_g[layer][None, :], ln_mix_b[layer][None, :],
                       ffn_w_gate[layer].astype(BF16), ffn_w_up[layer].astype(BF16),
                       ffn_w_down[layer].astype(BF16),
                       ln_ffn_g[layer][None, :], ln_ffn_b[layer][None, :])
    return h.reshape(BATCH, SEQ, D_MODEL)
```

```python
import functools

import numpy as np
import jax
import jax.numpy as jnp
from jax import lax
from jax.experimental import pallas as pl
from jax.experimental.pallas import tpu as pltpu

F32 = jnp.float32
BF16 = jnp.bfloat16

D_MODEL = 1024
BATCH = 2
SEQ = 16384
DEPTH = 4
TOKENS = BATCH * SEQ

CONV_WIDTH = 512
CONV_K = 3
SWA_HEADS = 8
SWA_KV_HEADS = 2
SWA_HEAD_DIM = 64
SWA_BLOCK = 128
GLA_HEADS = 4
GLA_DK = 64
GLA_DV = 128
GLA_GATE_RANK = 16
GLA_GATE_TAU = 16.0
GLA_CHUNK = 64
MLA_HEADS = 4
MLA_Q_RANK = 256
MLA_KV_RANK = 128
MLA_NOPE = 64
MLA_ROPE = 32
MLA_V = 128
ROPE_THETA = 10000.0
D_FF = 2816
ALPHA = (2.0 * DEPTH) ** 0.25
LN_EPS = 1e-5
RMS_EPS = 1e-6
LOG2E = 1.4426950408889634

LANES = 128
SUBLANES = 8
VMEM_LIMIT = 56 * 1024 * 1024

EVEN_IN = 3 * CONV_WIDTH + SWA_HEADS * SWA_HEAD_DIM + 2 * SWA_KV_HEADS * SWA_HEAD_DIM
GLA_QK = GLA_HEADS * GLA_DK
GLA_VW = GLA_HEADS * GLA_DV
MLA_W = MLA_HEADS * LANES
MLA_VT_ROWS = MLA_V + 16
ODD_EXT = 2 * GLA_QK + 2 * GLA_VW + MLA_Q_RANK + MLA_KV_RANK + 3 * LANES
GLA_LEVELS = 6
GLA_PROWS = 4 * GLA_CHUNK
GLA_GROUP = 8

TM_EVEN = 512
TM_ODD = 512
TM_POST = 512
TG_GLA = 512
TQ_MLA = 512
MLA_ROW_CHUNK = 64
MLA_RING = 4
MLA_UNROLL = 8
MLA_M_INIT = -3.0e38
TK_MLA = 512


def _params(*sem):
    return pltpu.CompilerParams(dimension_semantics=sem, vmem_limit_bytes=VMEM_LIMIT)


def _const_spec(shape):
    return pl.BlockSpec(shape, lambda *_: (0,) * len(shape), pipeline_mode=pl.Buffered(1))


def _layer_norm(v, g, b):
    mu = jnp.mean(v, axis=-1, keepdims=True)
    d = v - mu
    var = jnp.mean(d * d, axis=-1, keepdims=True)
    return d * lax.rsqrt(var + LN_EPS) * g + b


def _dot(a, b):
    return jnp.dot(a, b, preferred_element_type=F32)


def _dot_nt(a, b):
    return lax.dot_general(a, b, (((1,), (1,)), ((), ())), preferred_element_type=F32)


def _dot_tn(a, b):
    return lax.dot_general(a, b, (((0,), (0,)), ((), ())), preferred_element_type=F32)


def _post_kernel(x_ref, y1_ref, y2_ref, wo1_ref, wo2_ref, g1_ref, b1_ref,
                 wg_ref, wu_ref, wd_ref, g2_ref, b2_ref, o_ref):
    tm = x_ref.shape[0]
    halves = [slice(0, tm // 2), slice(tm // 2, tm)]
    ys = [_dot(y1_ref[r, :], wo1_ref[...]) + _dot(y2_ref[r, :], wo2_ref[...]) for r in halves]
    x1s, hs = [], []
    for r, y in zip(halves, ys):
        x1 = _layer_norm(ALPHA * x_ref[r, :] + y, g1_ref[...], b1_ref[...])
        xb = x1.astype(BF16)
        gate = _dot(xb, wg_ref[...])
        up = _dot(xb, wu_ref[...])
        x1s.append(x1)
        hs.append((gate * jax.nn.sigmoid(gate) * up).astype(BF16))
    fs = [_dot(h, wd_ref[...]) for h in hs]
    for r, x1, f in zip(halves, x1s, fs):
        o_ref[r, :] = _layer_norm(ALPHA * x1 + f, g2_ref[...], b2_ref[...])


def _post_call(x, y1, y2, wo1, wo2, g1, b1, wg, wu, wd, g2, b2):
    tm = TM_POST
    half = D_MODEL // 2
    row = lambda w: pl.BlockSpec((tm, w), lambda i: (i, 0))
    return pl.pallas_call(
        _post_kernel,
        grid=(TOKENS // tm,),
        in_specs=[row(D_MODEL), row(half), row(half),
                  _const_spec((half, D_MODEL)), _const_spec((half, D_MODEL)),
                  _const_spec((1, D_MODEL)), _const_spec((1, D_MODEL)),
                  _const_spec((D_MODEL, D_FF)), _const_spec((D_MODEL, D_FF)),
                  _const_spec((D_FF, D_MODEL)),
                  _const_spec((1, D_MODEL)), _const_spec((1, D_MODEL))],
        out_specs=row(D_MODEL),
        out_shape=jax.ShapeDtypeStruct((TOKENS, D_MODEL), F32),
        compiler_params=_params("parallel"),
        name="post_ffn",
    )(x, y1, y2, wo1, wo2, g1, b1, wg, wu, wd, g2, b2)


def _even_kernel(x_ref, win_ref, cw_ref, sink_ref, ya_ref, yb_ref, zc_ref, kvp_ref, *, tm):
    i = pl.program_id(1)

    @pl.when(i == 0)
    def _():
        zc_ref[...] = jnp.zeros_like(zc_ref)
        kvp_ref[...] = jnp.zeros_like(kvp_ref)

    cw = CONV_WIDTH
    blk = SWA_BLOCK
    half_rows = tm // 2
    blocks_per_half = half_rows // blk
    qoff = 3 * cw
    koff = qoff + SWA_HEADS * SWA_HEAD_DIM
    voff = koff + SWA_KV_HEADS * SWA_HEAD_DIM
    group = SWA_HEADS // SWA_KV_HEADS

    def project(hf):
        r0 = hf * half_rows
        u = _dot(x_ref[r0:r0 + half_rows, :].astype(BF16), win_ref[...])
        return dict(b_gate=u[:, 0:cw], z=u[:, cw:2 * cw] * u[:, 2 * cw:3 * cw],
                    q=u[:, qoff:koff] * (SWA_HEAD_DIM ** -0.5),
                    k=u[:, koff:voff].astype(BF16), v=u[:, voff:voff + LANES].astype(BF16))

    w0 = cw_ref[0:1, :]
    w1 = cw_ref[1:2, :]
    w2 = cw_ref[2:3, :]
    rid = lax.broadcasted_iota(jnp.int32, (SUBLANES, cw), 0)

    def conv_half(hf, part, c8):
        r0 = hf * half_rows
        z, b_gate = part["z"], part["b_gate"]
        r1 = pltpu.roll(z, 1, 0)
        r2 = pltpu.roll(z, 2, 0)
        top1 = jnp.where(rid < 1, pltpu.roll(c8, 1, 0), r1[0:SUBLANES])
        top2 = jnp.where(rid < 2, pltpu.roll(c8, 2, 0), r2[0:SUBLANES])
        y_top = top2 * w0 + top1 * w1 + z[0:SUBLANES] * w2
        y_rest = r2[SUBLANES:] * w0 + r1[SUBLANES:] * w1 + z[SUBLANES:] * w2
        ya_ref[r0:r0 + SUBLANES, :] = (b_gate[0:SUBLANES] * y_top).astype(BF16)
        ya_ref[r0 + SUBLANES:r0 + half_rows, :] = (b_gate[SUBLANES:] * y_rest).astype(BF16)
        return z[half_rows - SUBLANES:half_rows]

    r = lax.broadcasted_iota(jnp.int32, (blk, 2 * blk), 0)
    c = lax.broadcasted_iota(jnp.int32, (blk, 2 * blk), 1)
    own = (c >= blk) & ((c - blk) <= r)
    prev = (c < blk) & (c > r)
    prev_first = (c < blk) & (c > r + jnp.where(i > 0, 0, blk))
    lane = lax.broadcasted_iota(jnp.int32, (blk, LANES), 1)
    low_half = lane < SWA_HEAD_DIM
    mask_inner = own | prev
    mask_first = own | prev_first

    def block_scores(part, jl, k_prev, v_prev):
        rows = slice(jl * blk, (jl + 1) * blk)
        q, k, v = part["q"], part["k"], part["v"]
        kk = jnp.concatenate([k_prev, k[rows]], axis=0)
        vv = jnp.concatenate([v_prev, v[rows]], axis=0)
        scores = []
        for g in range(SWA_KV_HEADS):
            in_g = low_half if g == 0 else jnp.logical_not(low_half)
            qs = []
            for hi in range(group):
                h = g * group + hi
                qp = q[rows, (h // 2) * LANES:(h // 2 + 1) * LANES]
                src = qp if h % 2 == g else pltpu.roll(qp, SWA_HEAD_DIM, 1)
                qs.append(jnp.where(in_g, src, 0.0).astype(BF16))
            scores.append(_dot_nt(jnp.concatenate(qs, axis=0), kk))
        return scores, vv, k[rows], v[rows]

    def block_finish(j, scores, vv):
        rows = slice(j * blk, (j + 1) * blk)
        mask = mask_first if j == 0 else mask_inner
        ps, dens = [], []
        for g in range(SWA_KV_HEADS):
            ps_g, dens_g = [], []
            for hi in range(group):
                sink = sink_ref[g * group + hi]
                s = jnp.where(mask, scores[g][hi * blk:(hi + 1) * blk], -jnp.inf)
                m = jnp.maximum(jnp.max(s, axis=-1, keepdims=True), sink)
                p = jnp.exp(s - m)
                dens_g.append(jnp.sum(p, axis=-1, keepdims=True) + jnp.exp(sink - m))
                ps_g.append(p.astype(BF16))
            ps.append(jnp.concatenate(ps_g, axis=0))
            dens.append(dens_g)
        outs = [_dot(ps[g], vv) for g in range(SWA_KV_HEADS)]
        for g in range(SWA_KV_HEADS):
            for pair in range(group // 2):
                even = outs[g][(2 * pair) * blk:(2 * pair + 1) * blk] / dens[g][2 * pair]
                odd = outs[g][(2 * pair + 1) * blk:(2 * pair + 2) * blk] / dens[g][2 * pair + 1]
                if g == 0:
                    out = jnp.where(low_half, even, pltpu.roll(odd, SWA_HEAD_DIM, 1))
                else:
                    out = jnp.where(low_half, pltpu.roll(even, SWA_HEAD_DIM, 1), odd)
                col = (g * group // 2 + pair) * LANES
                yb_ref[rows, col:col + LANES] = out.astype(BF16)

    k_prev = kvp_ref[:, 0:LANES]
    v_prev = kvp_ref[:, LANES:2 * LANES]
    c8 = zc_ref[...]
    waiting = None
    for hf in range(3):
        if hf < 2:
            part = project(hf)
            scored = []
            for jl in range(blocks_per_half):
                scores, vv, k_prev, v_prev = block_scores(part, jl, k_prev, v_prev)
                scored.append((hf * blocks_per_half + jl, scores, vv))
        if waiting is not None:
            w_hf, w_part, w_scored = waiting
            c8 = conv_half(w_hf, w_part, c8)
            for j, scores, vv in w_scored:
                block_finish(j, scores, vv)
        waiting = (hf, part, scored) if hf < 2 else None
    zc_ref[...] = c8
    kvp_ref[:, 0:LANES] = k_prev
    kvp_ref[:, LANES:2 * LANES] = v_prev


def _even_call(x, w_in, conv_w, sinks):
    tm = TM_EVEN
    nt = SEQ // tm
    row = lambda w: pl.BlockSpec((tm, w), lambda b, i: (b * nt + i, 0))
    half = D_MODEL // 2
    return pl.pallas_call(
        functools.partial(_even_kernel, tm=tm),
        grid=(BATCH, nt),
        in_specs=[row(D_MODEL), _const_spec((D_MODEL, EVEN_IN)), _const_spec((CONV_K, CONV_WIDTH)),
                  pl.BlockSpec(memory_space=pltpu.SMEM)],
        out_specs=[row(half), row(half)],
        out_shape=[jax.ShapeDtypeStruct((TOKENS, half), BF16)] * 2,
        scratch_shapes=[pltpu.VMEM((SUBLANES, CONV_WIDTH), F32),
                        pltpu.VMEM((SWA_BLOCK, 2 * LANES), BF16)],
        compiler_params=_params("arbitrary", "arbitrary"),
        name="even_mixer",
    )(x, w_in, conv_w, sinks)


def _odd_proj_kernel(x_ref, win_ref, wgate_ref, bgate_ref, gq_ref, wq_ref, gkv_ref, wkv_ref,
                     ctab_ref, stab_ref,
                     gq_o, gk_o, gv_o, gr_o, la_o, qt_o, k_o, vt_o):
    tm = x_ref.shape[0]
    halves = [slice(0, tm // 2), slice(tm // 2, tm)]
    us = [_dot(x_ref[rows, :].astype(BF16), win_ref[...]) for rows in halves]
    for rows, u in zip(halves, us):
        n = rows.stop - rows.start
        o = 0
        gq_o[rows, :] = (u[:, o:o + GLA_QK] * (GLA_DK ** -0.5)).astype(BF16); o += GLA_QK
        gk_o[rows, :] = u[:, o:o + GLA_QK].astype(BF16); o += GLA_QK
        gv_o[rows, :] = u[:, o:o + GLA_VW].astype(BF16); o += GLA_VW
        gr_o[rows, :] = u[:, o:o + GLA_VW].astype(BF16); o += GLA_VW
        c_q = u[:, o:o + MLA_Q_RANK]; o += MLA_Q_RANK
        c_kv = u[:, o:o + MLA_KV_RANK]; o += MLA_KV_RANK
        g_low = u[:, o:o + LANES]; o += LANES
        kr = u[:, o:o + LANES]; o += LANES
        kr_sw = u[:, o:o + LANES]

        xg = _dot(g_low.astype(BF16), wgate_ref[...]) + bgate_ref[...]
        log_sig = jnp.minimum(xg, 0.0) - jnp.log(1.0 + jnp.exp(-jnp.abs(xg)))
        la_o[rows, :] = log_sig * (1.0 / GLA_GATE_TAU)

        ctab = ctab_ref[rows, :]
        stab = stab_ref[rows, :]
        cqn = c_q * lax.rsqrt(jnp.mean(c_q * c_q, axis=-1, keepdims=True) + RMS_EPS) * gq_ref[...]
        q2 = _dot(cqn.astype(BF16), wq_ref[...])
        scale = (MLA_NOPE + MLA_ROPE) ** -0.5 * LOG2E
        ckvn = c_kv * lax.rsqrt(jnp.mean(c_kv * c_kv, axis=-1, keepdims=True) + RMS_EPS) * gkv_ref[...]
        kv = _dot(ckvn.astype(BF16), wkv_ref[...])
        k_rope = kr * ctab + kr_sw * stab
        for h in range(MLA_HEADS):
            cols = slice(h * LANES, (h + 1) * LANES)
            qa = q2[:, cols]
            qb = q2[:, MLA_W + h * LANES:MLA_W + (h + 1) * LANES]
            qt_o[0, cols, rows] = ((qa * ctab + qb * stab) * scale).T.astype(BF16)
            k_o[rows, cols] = (kv[:, cols] + k_rope).astype(BF16)
            vrow = h * MLA_VT_ROWS
            vt_o[0, vrow:vrow + MLA_V, rows] = kv[:, MLA_W + h * LANES:MLA_W + (h + 1) * LANES].T.astype(BF16)
            vt_o[0, vrow + MLA_V:vrow + MLA_VT_ROWS, rows] = jnp.ones((MLA_VT_ROWS - MLA_V, n), BF16)


def _odd_proj_call(x, w_in, wgate, bgate, gq, wq, gkv, wkv, ctab, stab):
    tm = TM_ODD
    nt = SEQ // tm
    row = lambda w: pl.BlockSpec((tm, w), lambda i: (i, 0))
    tab = pl.BlockSpec((tm, LANES), lambda i: (i % nt, 0))
    tile_t = lambda r: pl.BlockSpec((1, r, tm), lambda i: (i, 0, 0))
    rows_out = [(GLA_QK, BF16), (GLA_QK, BF16), (GLA_VW, BF16), (GLA_VW, BF16), (GLA_QK, F32)]
    row_shape = lambda w, dt: jax.ShapeDtypeStruct((TOKENS, w), dt)
    tile_shape = lambda r: jax.ShapeDtypeStruct((TOKENS // tm, r, tm), BF16)
    vt_rows = MLA_HEADS * MLA_VT_ROWS
    return pl.pallas_call(
        _odd_proj_kernel,
        grid=(TOKENS // tm,),
        in_specs=[row(D_MODEL), _const_spec((D_MODEL, ODD_EXT)),
                  _const_spec((LANES, GLA_QK)), _const_spec((1, GLA_QK)),
                  _const_spec((1, MLA_Q_RANK)), _const_spec((MLA_Q_RANK, 2 * MLA_W)),
                  _const_spec((1, MLA_KV_RANK)), _const_spec((MLA_KV_RANK, 2 * MLA_W)),
                  tab, tab],
        out_specs=[row(w) for w, _ in rows_out] + [tile_t(MLA_W), row(MLA_W), tile_t(vt_rows)],
        out_shape=[row_shape(w, dt) for w, dt in rows_out]
        + [tile_shape(MLA_W), row_shape(MLA_W, BF16), tile_shape(vt_rows)],
        compiler_params=_params("parallel"),
        name="odd_proj",
    )(x, w_in, wgate, bgate, gq, wq, gkv, wkv, ctab, stab)


def _gla_prefix_matrix():
    c = GLA_CHUNK
    t = np.arange(c)[:, None]
    u = np.arange(c)[None, :]
    blocks = [(u <= t)]
    length = SUBLANES // 2
    while length >= 1:
        rho = (t // (2 * length)) * 2 * length + length - 1
        right = (t % (2 * length)) >= length
        blocks.append(np.where(right, (u > rho) & (u <= t), (u > t) & (u <= rho)))
        length //= 2
    return np.concatenate(blocks, axis=0).astype(np.float32)


def _gla_pair_masks():
    c = GLA_CHUNK
    t = np.arange(c)[:, None]
    s = np.arange(c)[None, :]
    masks = []
    length = c // 2
    while length >= 1:
        same = (t // (2 * length)) == (s // (2 * length))
        masks.append(same & ((t % (2 * length)) >= length) & ((s % (2 * length)) < length))
        length //= 2
    masks.append(t == s)
    m = np.stack(masks).astype(np.float32)
    return np.tile(m, (1, 1, GLA_HEADS))


def _gla_kernel(q_ref, k_ref, v_ref, r_ref, g_ref, pm_ref, mk_ref, gn_ref, o_ref, st_ref, *, tg):
    @pl.when(pl.program_id(1) == 0)
    def _():
        st_ref[...] = jnp.zeros_like(st_ref)

    c = GLA_CHUNK
    nh = GLA_HEADS
    lane_qk = lax.broadcasted_iota(jnp.int32, (c, GLA_QK), 1)
    head_of_lane = [(lane_qk >= h * GLA_DK) & (lane_qk < (h + 1) * GLA_DK) for h in range(nh)]
    zeros_v = jnp.zeros((c, GLA_DV), BF16)
    ones_c = jnp.ones((c, LANES), BF16)
    gnorm = gn_ref[...]

    def stack_heads(a):
        return jnp.concatenate([jnp.where(head_of_lane[h], a, 0.0) for h in range(nh)],
                               axis=0).astype(BF16)

    def block_diag(blocks):
        rows = []
        for h in range(nh):
            rows.append(jnp.concatenate(
                [blocks[h] if hh == h else zeros_v for hh in range(nh)], axis=1))
        return jnp.concatenate(rows, axis=0)

    def level_sums(b, length):
        parts = []
        for blk in range(0, c, 2 * length):
            b_rho = b[blk + length - 1:blk + length]
            parts.append(b_rho - b[blk:blk + length])
            parts.append(b[blk + length:blk + 2 * length] - b_rho)
        return jnp.concatenate(parts, axis=0)

    def chunk(ci):
        r0 = pl.multiple_of(ci * c, c)
        rows = pl.ds(r0, c)
        q = q_ref[rows, :].astype(F32)
        k = k_ref[rows, :].astype(F32)
        v = v_ref[rows, :]
        g = g_ref[rows, :]
        g_hi = g.astype(BF16)
        g_lo = (g - g_hi.astype(F32)).astype(BF16)
        g2 = jnp.concatenate([g_hi, g_lo], axis=1)
        x2 = _dot(pm_ref[...], g2)
        xs = x2[:, 0:GLA_QK] + x2[:, GLA_QK:2 * GLA_QK]
        b = xs[0:c]
        yield False
        qe = (q * jnp.exp(b)).astype(BF16)
        kd = (k * jnp.exp(b[c - 1:c] - b)).astype(BF16)
        upd = _dot_tn(kd, v)
        bcol2 = _dot_tn(g2, ones_c)
        yield False

        att = _dot_nt(q.astype(BF16), stack_heads(k)) * mk_ref[GLA_LEVELS]
        yield False
        small = GLA_LEVELS - (GLA_PROWS // c - 1)
        for lv in range(GLA_LEVELS):
            if lv < small:
                x_lv = level_sums(b, c >> (lv + 1))
            else:
                x_lv = xs[(1 + lv - small) * c:(2 + lv - small) * c]
            e = jnp.exp(x_lv)
            att = att + _dot_nt((q * e).astype(BF16), stack_heads(k * e)) * mk_ref[lv]
            yield False

        v_heads = [v[:, h * GLA_DV:(h + 1) * GLA_DV] for h in range(nh)]
        lhs = jnp.concatenate([att.astype(BF16), qe], axis=1)
        vbd = block_diag(v_heads)
        dcol = jnp.exp(bcol2[0:GLA_QK] + bcol2[GLA_QK:2 * GLA_QK])
        rr = r_ref[rows, :].astype(F32)
        gate = rr * jax.nn.sigmoid(rr)
        yield True

        s_heads = [st_ref[h].astype(BF16) for h in range(nh)]
        rhs = jnp.concatenate([vbd, block_diag(s_heads)], axis=0)
        o = _dot(lhs, rhs)
        for h in range(nh):
            rs = slice(h * GLA_DK, (h + 1) * GLA_DK)
            st_ref[h] = dcol[rs] * st_ref[h] + upd[rs, h * GLA_DV:(h + 1) * GLA_DV]
        for h in range(nh):
            cs = slice(h * GLA_DV, (h + 1) * GLA_DV)
            oh = o[:, cs]
            ms = jnp.mean(oh * oh, axis=-1, keepdims=True)
            o_ref[rows, cs] = (oh * lax.rsqrt(ms + RMS_EPS) * gnorm * gate[:, cs]).astype(BF16)

    def chunk_group(gi, carry):
        gens = [chunk(gi * GLA_GROUP + i) for i in range(GLA_GROUP)]
        live = list(gens)
        while live:
            for gen in list(live):
                if next(gen):
                    live.remove(gen)
        for gen in gens:
            for _ in gen:
                pass
        return carry

    lax.fori_loop(0, tg // (c * GLA_GROUP), chunk_group, 0)


def _gla_call(gq, gk, gv, gr, la, pm, mk, gn):
    tg = TG_GLA
    nt = SEQ // tg
    row = lambda w: pl.BlockSpec((tg, w), lambda b, i: (b * nt + i, 0))
    return pl.pallas_call(
        functools.partial(_gla_kernel, tg=tg),
        grid=(BATCH, nt),
        in_specs=[row(GLA_QK), row(GLA_QK), row(GLA_VW), row(GLA_VW), row(GLA_QK),
                  _const_spec((GLA_PROWS, GLA_CHUNK)),
                  _const_spec((GLA_LEVELS + 1, GLA_CHUNK, GLA_HEADS * GLA_CHUNK)),
                  _const_spec((1, GLA_DV))],
        out_specs=row(GLA_VW),
        out_shape=jax.ShapeDtypeStruct((TOKENS, GLA_VW), BF16),
        scratch_shapes=[pltpu.VMEM((GLA_HEADS, GLA_DK, GLA_DV), F32)],
        compiler_params=_params("arbitrary", "arbitrary"),
        name="gla_mixer",
    )(gq, gk, gv, gr, la, pm, mk, gn)


def _mla_kernel(qt_ref, k_ref, vt_ref, o_ref, m_ref, acc_ref, st_ref, mx_ref, p_ref, a_ref, *, tile):
    qi = pl.program_id(1)
    qw = tile // 2
    chains = [(h, qh) for h in range(MLA_HEADS) for qh in range(2)]
    n = len(chains)
    ring = MLA_RING
    chunk = MLA_ROW_CHUNK

    def lanes_of(qh):
        return slice(qh * qw, (qh + 1) * qw)

    def scores(j, c, diagonal):
        h, qh = chains[c]
        cols = slice(h * LANES, (h + 1) * LANES)
        k0 = pl.multiple_of(j * tile, tile)
        st = _dot(k_ref[pl.ds(k0, tile), cols], qt_ref[0, cols, lanes_of(qh)])
        if diagonal:
            r = lax.broadcasted_iota(jnp.int32, st.shape, 0)
            col = lax.broadcasted_iota(jnp.int32, st.shape, 1) + qh * qw
            st = jnp.where(r <= col, st, -jnp.inf)
        st_ref[c % ring] = st

    def max_pass(c):
        slot = c % ring
        mx = None
        for r0 in range(0, tile, chunk):
            blk = st_ref[slot, r0:r0 + chunk, :]
            mx = blk if mx is None else jnp.maximum(mx, blk)
            yield
        mx_ref[slot] = jnp.max(mx, axis=0, keepdims=True)

    def exp_pass(c):
        h, qh = chains[c]
        slot = c % ring
        m_prev = m_ref[h, :, lanes_of(qh)]
        m_new = jnp.maximum(m_prev, mx_ref[slot])
        m_ref[h, :, lanes_of(qh)] = m_new
        a_ref[slot] = jnp.exp2(m_prev - m_new)
        for r0 in range(0, tile, chunk):
            p_ref[slot, r0:r0 + chunk, :] = jnp.exp2(st_ref[slot, r0:r0 + chunk, :] - m_new).astype(BF16)
            yield

    def alternate(*gens):
        live = list(gens)
        while live:
            for g in list(live):
                try:
                    next(g)
                except StopIteration:
                    live.remove(g)

    def weighted_values(j, c):
        h, qh = chains[c]
        slot = c % ring
        rows = slice(h * MLA_VT_ROWS, (h + 1) * MLA_VT_ROWS)
        pv = _dot(vt_ref[j, rows, :], p_ref[slot])
        acc_ref[h, :, lanes_of(qh)] = a_ref[slot] * acc_ref[h, :, lanes_of(qh)] + pv

    def pipeline_steps(j, steps, diagonal):
        j_prev = jnp.maximum(j - 1, 0)
        for t in steps:
            if t < n:
                scores(j, t, diagonal)
            gens = []
            if t - 1 < n:
                gens.append(max_pass((t - 1) % n))
            if t - 2 < n:
                gens.append(exp_pass((t - 2) % n))
            alternate(*gens)
            if t - 3 < n:
                weighted_values(j if t >= 3 else j_prev, (t - 3) % n)

    m_ref[...] = jnp.full_like(m_ref, MLA_M_INIT)
    acc_ref[...] = jnp.zeros_like(acc_ref)
    st_ref[(n - 1) % ring] = jnp.full((tile, qw), -jnp.inf, F32)
    st_ref[(n - 2) % ring] = jnp.full((tile, qw), -jnp.inf, F32)
    mx_ref[(n - 2) % ring] = jnp.full((1, qw), -jnp.inf, F32)
    p_ref[(n - 3) % ring] = jnp.zeros((tile, qw), BF16)
    a_ref[(n - 3) % ring] = jnp.ones((1, qw), F32)

    def tiles(first, count):
        for u in range(count):
            pipeline_steps(first + u, range(n), False)

    def body(i, carry):
        tiles(MLA_UNROLL * i, MLA_UNROLL)
        return carry

    lax.fori_loop(0, qi // MLA_UNROLL, body, 0)

    rem = qi % MLA_UNROLL
    run = MLA_UNROLL // 2
    while run >= 1:
        @pl.when((rem // run) % 2 == 1)
        def _():
            tiles(qi - rem % (2 * run), run)
        run //= 2

    pipeline_steps(qi, range(n), True)
    pipeline_steps(qi, range(n, n + 3), True)

    for h in range(MLA_HEADS):
        num = acc_ref[h, 0:MLA_V, :]
        den = acc_ref[h, MLA_V:MLA_V + 1, :]
        o_ref[:, h * LANES:(h + 1) * LANES] = (num / den).T.astype(BF16)


def _mla_call(qt, k, vt):
    tile = TQ_MLA
    assert tile == TK_MLA == TM_ODD
    nt = SEQ // tile
    qw = tile // 2
    vt_rows = MLA_HEADS * MLA_VT_ROWS
    once = pl.Buffered(1)
    return pl.pallas_call(
        functools.partial(_mla_kernel, tile=tile),
        grid=(BATCH, nt),
        in_specs=[pl.BlockSpec((1, MLA_W, tile), lambda b, i: (b * nt + i, 0, 0)),
                  pl.BlockSpec((SEQ, MLA_W), lambda b, i: (b, 0), pipeline_mode=once),
                  pl.BlockSpec((nt, vt_rows, tile), lambda b, i: (b, 0, 0), pipeline_mode=once)],
        out_specs=pl.BlockSpec((tile, MLA_W), lambda b, i: (b * nt + i, 0)),
        out_shape=jax.ShapeDtypeStruct((TOKENS, MLA_W), BF16),
        scratch_shapes=[pltpu.VMEM((MLA_HEADS, 1, tile), F32),
                        pltpu.VMEM((MLA_HEADS, MLA_VT_ROWS, tile), F32),
                        pltpu.VMEM((MLA_RING, tile, qw), F32),
                        pltpu.VMEM((MLA_RING, 1, qw), F32),
                        pltpu.VMEM((MLA_RING, tile, qw), BF16),
                        pltpu.VMEM((MLA_RING, 1, qw), F32)],
        compiler_params=_params("parallel", "arbitrary"),
        name="mla_attn",
    )(qt, k, vt)


def _rope_tables(dtype):
    pos = jnp.arange(SEQ, dtype=F32)
    inv_freq = ROPE_THETA ** (-jnp.arange(0, MLA_ROPE, 2, dtype=F32) / MLA_ROPE)
    ang = pos[:, None] * inv_freq[None, :]
    cos, sin = jnp.cos(ang).astype(dtype), jnp.sin(ang).astype(dtype)
    zeros = jnp.zeros((SEQ, LANES - MLA_NOPE - MLA_ROPE), dtype)
    ctab = jnp.concatenate([jnp.ones((SEQ, MLA_NOPE), dtype), cos, cos, zeros], axis=1)
    stab = jnp.concatenate([jnp.zeros((SEQ, MLA_NOPE), dtype), -sin, sin, zeros], axis=1)
    return ctab, stab


def _swap_halves(w):
    half = w.shape[-1] // 2
    return jnp.concatenate([w[..., half:], w[..., :half]], axis=-1)


def _place_rope(w):
    rows = w.shape[0]
    return jnp.concatenate([jnp.zeros((rows, MLA_NOPE), w.dtype), w,
                            jnp.zeros((rows, LANES - MLA_NOPE - MLA_ROPE), w.dtype)], axis=1)


def _odd_weights(w_in, w_gate, w_uq, w_ukv):
    o = 0
    gq = w_in[:, o:o + GLA_QK]; o += GLA_QK
    gk = w_in[:, o:o + GLA_QK]; o += GLA_QK
    gv = w_in[:, o:o + GLA_VW]; o += GLA_VW
    g_low = w_in[:, o:o + GLA_GATE_RANK]; o += GLA_GATE_RANK
    gr = w_in[:, o:o + GLA_VW]; o += GLA_VW
    c_q = w_in[:, o:o + MLA_Q_RANK]; o += MLA_Q_RANK
    c_kv = w_in[:, o:o + MLA_KV_RANK]; o += MLA_KV_RANK
    k_r = w_in[:, o:o + MLA_ROPE]
    pad = jnp.zeros((D_MODEL, LANES - GLA_GATE_RANK), w_in.dtype)
    w_ext = jnp.concatenate([gq, gk, gv, gr, c_q, c_kv, g_low, pad,
                             _place_rope(k_r), _place_rope(_swap_halves(k_r))], axis=1)
    wgate = jnp.concatenate([w_gate, jnp.zeros((LANES - GLA_GATE_RANK, GLA_QK), w_gate.dtype)], axis=0)

    per_q = MLA_NOPE + MLA_ROPE
    qa, qb = [], []
    zq = jnp.zeros((MLA_Q_RANK, LANES - per_q), w_uq.dtype)
    for h in range(MLA_HEADS):
        nope = w_uq[:, h * per_q:h * per_q + MLA_NOPE]
        rope = w_uq[:, h * per_q + MLA_NOPE:(h + 1) * per_q]
        qa.append(jnp.concatenate([nope, rope, zq], axis=1))
        qb.append(_place_rope(_swap_halves(rope)))
    wq = jnp.concatenate(qa + qb, axis=1)

    per_kv = MLA_NOPE + MLA_V
    ks, vs = [], []
    zk = jnp.zeros((MLA_KV_RANK, LANES - MLA_NOPE), w_ukv.dtype)
    for h in range(MLA_HEADS):
        ks.append(jnp.concatenate([w_ukv[:, h * per_kv:h * per_kv + MLA_NOPE], zk], axis=1))
        vs.append(w_ukv[:, h * per_kv + MLA_NOPE:(h + 1) * per_kv])
    wkv = jnp.concatenate(ks + vs, axis=1)
    return w_ext.astype(BF16), wgate.astype(BF16), wq.astype(BF16), wkv.astype(BF16)


def kernel(x, ev_w_in, ev_conv_w, ev_sinks, ev_w_out, od_w_in, od_gla_w_gate, od_gla_b_gate,
           od_gla_norm_g, od_mla_q_norm_g, od_mla_w_uq, od_mla_kv_norm_g, od_mla_w_ukv, od_w_out,
           ffn_w_gate, ffn_w_up, ffn_w_down, ln_mix_g, ln_mix_b, ln_ffn_g, ln_ffn_b):
    assert x.shape == (BATCH, SEQ, D_MODEL) and x.dtype == F32
    ctab, stab = _rope_tables(x.dtype)
    pm = jnp.asarray(_gla_prefix_matrix(), BF16)
    mk = jnp.asarray(_gla_pair_masks(), F32)
    half = D_MODEL // 2
    h = x.reshape(TOKENS, D_MODEL)
    for layer in range(DEPTH):
        i = layer // 2
        if layer % 2 == 0:
            y1, y2 = _even_call(h, ev_w_in[i].astype(BF16), ev_conv_w[i], ev_sinks[i])
            w_out = ev_w_out[i]
        else:
            w_ext, wgate, wq, wkv = _odd_weights(od_w_in[i], od_gla_w_gate[i],
                                                 od_mla_w_uq[i], od_mla_w_ukv[i])
            gq, gk, gv, gr, la, mq, mk_, mv = _odd_proj_call(
                h, w_ext, wgate, od_gla_b_gate[i][None, :], od_mla_q_norm_g[i][None, :], wq,
                od_mla_kv_norm_g[i][None, :], wkv, ctab, stab)
            y1 = _gla_call(gq, gk, gv, gr, la, pm, mk, od_gla_norm_g[i][None, :])
            y2 = _mla_call(mq, mk_, mv)
            w_out = od_w_out[i]
        h = _post_call(h, y1, y2, w_out[:half].astype(BF16), w_out[half:].astype(BF16),
                       ln_mix_g[layer][None, :], ln_mix_b[layer][None, :],
                       ffn_w_gate[layer].astype(BF16), ffn_w_up[layer].astype(BF16),
                       ffn_w_down[layer].astype(BF16),
                       ln_ffn_g[layer][None, :], ln_ffn_b[layer][None, :])
    return h.reshape(BATCH, SEQ, D_MODEL)
```

```python
import functools

import numpy as np
import jax
import jax.numpy as jnp
from jax import lax
from jax.experimental import pallas as pl
from jax.experimental.pallas import tpu as pltpu

F32 = jnp.float32
BF16 = jnp.bfloat16

D_MODEL = 1024
BATCH = 2
SEQ = 16384
DEPTH = 4
TOKENS = BATCH * SEQ

CONV_WIDTH = 512
CONV_K = 3
SWA_HEADS = 8
SWA_KV_HEADS = 2
SWA_HEAD_DIM = 64
SWA_BLOCK = 128
GLA_HEADS = 4
GLA_DK = 64
GLA_DV = 128
GLA_GATE_RANK = 16
GLA_GATE_TAU = 16.0
GLA_CHUNK = 64
MLA_HEADS = 4
MLA_Q_RANK = 256
MLA_KV_RANK = 128
MLA_NOPE = 64
MLA_ROPE = 32
MLA_V = 128
ROPE_THETA = 10000.0
D_FF = 2816
ALPHA = (2.0 * DEPTH) ** 0.25
LN_EPS = 1e-5
RMS_EPS = 1e-6
LOG2E = 1.4426950408889634

LANES = 128
SUBLANES = 8
VMEM_LIMIT = 56 * 1024 * 1024

EVEN_IN = 3 * CONV_WIDTH + SWA_HEADS * SWA_HEAD_DIM + 2 * SWA_KV_HEADS * SWA_HEAD_DIM
GLA_QK = GLA_HEADS * GLA_DK
GLA_VW = GLA_HEADS * GLA_DV
MLA_W = MLA_HEADS * LANES
MLA_VT_ROWS = MLA_V + 16
ODD_EXT = 2 * GLA_QK + 2 * GLA_VW + MLA_Q_RANK + MLA_KV_RANK + 3 * LANES
GLA_LEVELS = 6
GLA_PROWS = 4 * GLA_CHUNK
GLA_GROUP = 8

TM_EVEN = 512
TM_ODD = 512
ODD_PROJ_PARTS = 1
TM_POST = 512
TG_GLA = 512
TQ_MLA = 512
MLA_ROW_CHUNK = 64
MLA_RING = 4
MLA_UNROLL = 8
MLA_M_INIT = -3.0e38
TK_MLA = 512


def _params(*sem):
    return pltpu.CompilerParams(dimension_semantics=sem, vmem_limit_bytes=VMEM_LIMIT)


def _const_spec(shape):
    return pl.BlockSpec(shape, lambda *_: (0,) * len(shape), pipeline_mode=pl.Buffered(1))


def _layer_norm(v, g, b):
    mu = jnp.mean(v, axis=-1, keepdims=True)
    d = v - mu
    var = jnp.mean(d * d, axis=-1, keepdims=True)
    return d * lax.rsqrt(var + LN_EPS) * g + b


def _dot(a, b):
    return jnp.dot(a, b, preferred_element_type=F32)


def _dot_nt(a, b):
    return lax.dot_general(a, b, (((1,), (1,)), ((), ())), preferred_element_type=F32)


def _dot_tn(a, b):
    return lax.dot_general(a, b, (((0,), (0,)), ((), ())), preferred_element_type=F32)


def _post_mix(y1, y2, wo1_ref, wo2_ref):
    return _dot(y1, wo1_ref[...]) + _dot(y2, wo2_ref[...])


def _post_up(x, y, g1_ref, b1_ref, wg_ref, wu_ref):
    x1 = _layer_norm(ALPHA * x + y, g1_ref[...], b1_ref[...])
    xb = x1.astype(BF16)
    gate = _dot(xb, wg_ref[...])
    up = _dot(xb, wu_ref[...])
    return x1, (gate * jax.nn.sigmoid(gate) * up).astype(BF16)


def _post_kernel(x_ref, y1_ref, y2_ref, wo1_ref, wo2_ref, g1_ref, b1_ref,
                 wg_ref, wu_ref, wd_ref, g2_ref, b2_ref, o_ref):
    tm = x_ref.shape[0]
    halves = [slice(0, tm // 2), slice(tm // 2, tm)]
    ys = [_post_mix(y1_ref[r, :], y2_ref[r, :], wo1_ref, wo2_ref) for r in halves]
    ups = [_post_up(x_ref[r, :], y, g1_ref, b1_ref, wg_ref, wu_ref) for r, y in zip(halves, ys)]
    fs = [_dot(h, wd_ref[...]) for _, h in ups]
    for r, (x1, _), f in zip(halves, ups, fs):
        o_ref[r, :] = _layer_norm(ALPHA * x1 + f, g2_ref[...], b2_ref[...])


def _post_call(x, y1, y2, wo1, wo2, g1, b1, wg, wu, wd, g2, b2):
    tm = TM_POST
    half = D_MODEL // 2
    row = lambda w: pl.BlockSpec((tm, w), lambda i: (i, 0))
    return pl.pallas_call(
        _post_kernel,
        grid=(TOKENS // tm,),
        in_specs=[row(D_MODEL), row(half), row(half),
                  _const_spec((half, D_MODEL)), _const_spec((half, D_MODEL)),
                  _const_spec((1, D_MODEL)), _const_spec((1, D_MODEL)),
                  _const_spec((D_MODEL, D_FF)), _const_spec((D_MODEL, D_FF)),
                  _const_spec((D_FF, D_MODEL)),
                  _const_spec((1, D_MODEL)), _const_spec((1, D_MODEL))],
        out_specs=row(D_MODEL),
        out_shape=jax.ShapeDtypeStruct((TOKENS, D_MODEL), F32),
        compiler_params=_params("parallel"),
        name="post_ffn",
    )(x, y1, y2, wo1, wo2, g1, b1, wg, wu, wd, g2, b2)


def _even_kernel(x_ref, win_ref, cw_ref, sink_ref, ya_ref, yb_ref, zc_ref, kvp_ref, *, tm):
    i = pl.program_id(1)

    @pl.when(i == 0)
    def _():
        zc_ref[...] = jnp.zeros_like(zc_ref)
        kvp_ref[...] = jnp.zeros_like(kvp_ref)

    cw = CONV_WIDTH
    blk = SWA_BLOCK
    half_rows = tm // 2
    blocks_per_half = half_rows // blk
    qoff = 3 * cw
    koff = qoff + SWA_HEADS * SWA_HEAD_DIM
    voff = koff + SWA_KV_HEADS * SWA_HEAD_DIM
    group = SWA_HEADS // SWA_KV_HEADS

    def project(hf):
        r0 = hf * half_rows
        u = _dot(x_ref[r0:r0 + half_rows, :].astype(BF16), win_ref[...])
        return dict(b_gate=u[:, 0:cw], z=u[:, cw:2 * cw] * u[:, 2 * cw:3 * cw],
                    q=u[:, qoff:koff] * (SWA_HEAD_DIM ** -0.5),
                    k=u[:, koff:voff].astype(BF16), v=u[:, voff:voff + LANES].astype(BF16))

    w0 = cw_ref[0:1, :]
    w1 = cw_ref[1:2, :]
    w2 = cw_ref[2:3, :]
    rid = lax.broadcasted_iota(jnp.int32, (SUBLANES, cw), 0)

    def conv_half(hf, part, c8):
        r0 = hf * half_rows
        z, b_gate = part["z"], part["b_gate"]
        r1 = pltpu.roll(z, 1, 0)
        r2 = pltpu.roll(z, 2, 0)
        top1 = jnp.where(rid < 1, pltpu.roll(c8, 1, 0), r1[0:SUBLANES])
        top2 = jnp.where(rid < 2, pltpu.roll(c8, 2, 0), r2[0:SUBLANES])
        y_top = top2 * w0 + top1 * w1 + z[0:SUBLANES] * w2
        y_rest = r2[SUBLANES:] * w0 + r1[SUBLANES:] * w1 + z[SUBLANES:] * w2
        ya_ref[r0:r0 + SUBLANES, :] = (b_gate[0:SUBLANES] * y_top).astype(BF16)
        ya_ref[r0 + SUBLANES:r0 + half_rows, :] = (b_gate[SUBLANES:] * y_rest).astype(BF16)
        return z[half_rows - SUBLANES:half_rows]

    r = lax.broadcasted_iota(jnp.int32, (blk, 2 * blk), 0)
    c = lax.broadcasted_iota(jnp.int32, (blk, 2 * blk), 1)
    own = (c >= blk) & ((c - blk) <= r)
    prev = (c < blk) & (c > r)
    prev_first = (c < blk) & (c > r + jnp.where(i > 0, 0, blk))
    lane = lax.broadcasted_iota(jnp.int32, (blk, LANES), 1)
    low_half = lane < SWA_HEAD_DIM
    mask_inner = own | prev
    mask_first = own | prev_first

    def block_scores(part, jl, k_prev, v_prev):
        rows = slice(jl * blk, (jl + 1) * blk)
        q, k, v = part["q"], part["k"], part["v"]
        kk = jnp.concatenate([k_prev, k[rows]], axis=0)
        vv = jnp.concatenate([v_prev, v[rows]], axis=0)
        scores = []
        for g in range(SWA_KV_HEADS):
            in_g = low_half if g == 0 else jnp.logical_not(low_half)
            qs = []
            for hi in range(group):
                h = g * group + hi
                qp = q[rows, (h // 2) * LANES:(h // 2 + 1) * LANES]
                src = qp if h % 2 == g else pltpu.roll(qp, SWA_HEAD_DIM, 1)
                qs.append(jnp.where(in_g, src, 0.0).astype(BF16))
            scores.append(_dot_nt(jnp.concatenate(qs, axis=0), kk))
        return scores, vv, k[rows], v[rows]

    def block_finish(j, scores, vv):
        rows = slice(j * blk, (j + 1) * blk)
        mask = mask_first if j == 0 else mask_inner
        ps, dens = [], []
        for g in range(SWA_KV_HEADS):
            ps_g, dens_g = [], []
            for hi in range(group):
                sink = sink_ref[g * group + hi]
                s = jnp.where(mask, scores[g][hi * blk:(hi + 1) * blk], -jnp.inf)
                m = jnp.maximum(jnp.max(s, axis=-1, keepdims=True), sink)
                p = jnp.exp(s - m)
                dens_g.append(jnp.sum(p, axis=-1, keepdims=True) + jnp.exp(sink - m))
                ps_g.append(p.astype(BF16))
            ps.append(jnp.concatenate(ps_g, axis=0))
            dens.append(dens_g)
        outs = [_dot(ps[g], vv) for g in range(SWA_KV_HEADS)]
        for g in range(SWA_KV_HEADS):
            for pair in range(group // 2):
                even = outs[g][(2 * pair) * blk:(2 * pair + 1) * blk] / dens[g][2 * pair]
                odd = outs[g][(2 * pair + 1) * blk:(2 * pair + 2) * blk] / dens[g][2 * pair + 1]
                if g == 0:
                    out = jnp.where(low_half, even, pltpu.roll(odd, SWA_HEAD_DIM, 1))
                else:
                    out = jnp.where(low_half, pltpu.roll(even, SWA_HEAD_DIM, 1), odd)
                col = (g * group // 2 + pair) * LANES
                yb_ref[rows, col:col + LANES] = out.astype(BF16)

    k_prev = kvp_ref[:, 0:LANES]
    v_prev = kvp_ref[:, LANES:2 * LANES]
    c8 = zc_ref[...]
    waiting = None
    for hf in range(3):
        if hf < 2:
            part = project(hf)
            scored = []
            for jl in range(blocks_per_half):
                scores, vv, k_prev, v_prev = block_scores(part, jl, k_prev, v_prev)
                scored.append((hf * blocks_per_half + jl, scores, vv))
        if waiting is not None:
            w_hf, w_part, w_scored = waiting
            c8 = conv_half(w_hf, w_part, c8)
            for j, scores, vv in w_scored:
                block_finish(j, scores, vv)
        waiting = (hf, part, scored) if hf < 2 else None
    zc_ref[...] = c8
    kvp_ref[:, 0:LANES] = k_prev
    kvp_ref[:, LANES:2 * LANES] = v_prev


def _even_call(x, w_in, conv_w, sinks):
    tm = TM_EVEN
    nt = SEQ // tm
    row = lambda w: pl.BlockSpec((tm, w), lambda b, i: (b * nt + i, 0))
    half = D_MODEL // 2
    return pl.pallas_call(
        functools.partial(_even_kernel, tm=tm),
        grid=(BATCH, nt),
        in_specs=[row(D_MODEL), _const_spec((D_MODEL, EVEN_IN)), _const_spec((CONV_K, CONV_WIDTH)),
                  pl.BlockSpec(memory_space=pltpu.SMEM)],
        out_specs=[row(half), row(half)],
        out_shape=[jax.ShapeDtypeStruct((TOKENS, half), BF16)] * 2,
        scratch_shapes=[pltpu.VMEM((SUBLANES, CONV_WIDTH), F32),
                        pltpu.VMEM((SWA_BLOCK, 2 * LANES), BF16)],
        compiler_params=_params("arbitrary", "arbitrary"),
        name="even_mixer",
    )(x, w_in, conv_w, sinks)


def _odd_proj_kernel(x_ref, win_ref, wgate_ref, bgate_ref, gq_ref, wq_ref, gkv_ref, wkv_ref,
                     ctab_ref, stab_ref,
                     gq_o, gk_o, gv_o, gr_o, la_o, qt_o, k_o, vt_o):
    tm = x_ref.shape[0]
    n = tm // ODD_PROJ_PARTS
    for rows in [slice(r0, r0 + n) for r0 in range(0, tm, n)]:
        u = _dot(x_ref[rows, :].astype(BF16), win_ref[...])
        o = 0
        gq_o[rows, :] = (u[:, o:o + GLA_QK] * (GLA_DK ** -0.5)).astype(BF16); o += GLA_QK
        gk_o[rows, :] = u[:, o:o + GLA_QK].astype(BF16); o += GLA_QK
        gv_o[rows, :] = u[:, o:o + GLA_VW].astype(BF16); o += GLA_VW
        gr_o[rows, :] = u[:, o:o + GLA_VW].astype(BF16); o += GLA_VW
        c_q = u[:, o:o + MLA_Q_RANK]; o += MLA_Q_RANK
        c_kv = u[:, o:o + MLA_KV_RANK]; o += MLA_KV_RANK
        g_low = u[:, o:o + LANES]; o += LANES
        kr = u[:, o:o + LANES]; o += LANES
        kr_sw = u[:, o:o + LANES]

        xg = _dot(g_low.astype(BF16), wgate_ref[...]) + bgate_ref[...]
        log_sig = jnp.minimum(xg, 0.0) - jnp.log(1.0 + jnp.exp(-jnp.abs(xg)))
        la_o[rows, :] = log_sig * (1.0 / GLA_GATE_TAU)

        ctab = ctab_ref[rows, :]
        stab = stab_ref[rows, :]
        cqn = c_q * lax.rsqrt(jnp.mean(c_q * c_q, axis=-1, keepdims=True) + RMS_EPS) * gq_ref[...]
        q2 = _dot(cqn.astype(BF16), wq_ref[...])
        scale = (MLA_NOPE + MLA_ROPE) ** -0.5 * LOG2E
        ckvn = c_kv * lax.rsqrt(jnp.mean(c_kv * c_kv, axis=-1, keepdims=True) + RMS_EPS) * gkv_ref[...]
        kv = _dot(ckvn.astype(BF16), wkv_ref[...])
        k_rope = kr * ctab + kr_sw * stab
        for h in range(MLA_HEADS):
            cols = slice(h * LANES, (h + 1) * LANES)
            qa = q2[:, cols]
            qb = q2[:, MLA_W + h * LANES:MLA_W + (h + 1) * LANES]
            qt_o[0, cols, rows] = ((qa * ctab + qb * stab) * scale).T.astype(BF16)
            k_o[rows, cols] = (kv[:, cols] + k_rope).astype(BF16)
            vrow = h * MLA_VT_ROWS
            vt_o[0, vrow:vrow + MLA_V, rows] = kv[:, MLA_W + h * LANES:MLA_W + (h + 1) * LANES].T.astype(BF16)
            vt_o[0, vrow + MLA_V:vrow + MLA_VT_ROWS, rows] = jnp.ones((MLA_VT_ROWS - MLA_V, n), BF16)


def _odd_proj_call(x, w_in, wgate, bgate, gq, wq, gkv, wkv, ctab, stab):
    tm = TM_ODD
    nt = SEQ // tm
    row = lambda w: pl.BlockSpec((tm, w), lambda i: (i, 0))
    tab = pl.BlockSpec((tm, LANES), lambda i: (i % nt, 0))
    tile_t = lambda r: pl.BlockSpec((1, r, tm), lambda i: (i, 0, 0))
    rows_out = [(GLA_QK, BF16), (GLA_QK, BF16), (GLA_VW, BF16), (GLA_VW, BF16), (GLA_QK, F32)]
    row_shape = lambda w, dt: jax.ShapeDtypeStruct((TOKENS, w), dt)
    tile_shape = lambda r: jax.ShapeDtypeStruct((TOKENS // tm, r, tm), BF16)
    vt_rows = MLA_HEADS * MLA_VT_ROWS
    return pl.pallas_call(
        _odd_proj_kernel,
        grid=(TOKENS // tm,),
        in_specs=[row(D_MODEL), _const_spec((D_MODEL, ODD_EXT)),
                  _const_spec((LANES, GLA_QK)), _const_spec((1, GLA_QK)),
                  _const_spec((1, MLA_Q_RANK)), _const_spec((MLA_Q_RANK, 2 * MLA_W)),
                  _const_spec((1, MLA_KV_RANK)), _const_spec((MLA_KV_RANK, 2 * MLA_W)),
                  tab, tab],
        out_specs=[row(w) for w, _ in rows_out] + [tile_t(MLA_W), row(MLA_W), tile_t(vt_rows)],
        out_shape=[row_shape(w, dt) for w, dt in rows_out]
        + [tile_shape(MLA_W), row_shape(MLA_W, BF16), tile_shape(vt_rows)],
        compiler_params=_params("parallel"),
        name="odd_proj",
    )(x, w_in, wgate, bgate, gq, wq, gkv, wkv, ctab, stab)


def _gla_prefix_matrix():
    c = GLA_CHUNK
    t = np.arange(c)[:, None]
    u = np.arange(c)[None, :]
    blocks = [(u <= t)]
    length = SUBLANES // 2
    while length >= 1:
        rho = (t // (2 * length)) * 2 * length + length - 1
        right = (t % (2 * length)) >= length
        blocks.append(np.where(right, (u > rho) & (u <= t), (u > t) & (u <= rho)))
        length //= 2
    return np.concatenate(blocks, axis=0).astype(np.float32)


def _gla_pair_masks():
    c = GLA_CHUNK
    t = np.arange(c)[:, None]
    s = np.arange(c)[None, :]
    masks = []
    length = c // 2
    while length >= 1:
        same = (t // (2 * length)) == (s // (2 * length))
        masks.append(same & ((t % (2 * length)) >= length) & ((s % (2 * length)) < length))
        length //= 2
    masks.append(t == s)
    m = np.stack(masks).astype(np.float32)
    return np.tile(m, (1, 1, GLA_HEADS))


def _gla_kernel(q_ref, k_ref, v_ref, r_ref, g_ref, pm_ref, mk_ref, gn_ref, o_ref, st_ref, *, tg):
    @pl.when(pl.program_id(1) == 0)
    def _():
        st_ref[...] = jnp.zeros_like(st_ref)

    c = GLA_CHUNK
    nh = GLA_HEADS
    lane_qk = lax.broadcasted_iota(jnp.int32, (c, GLA_QK), 1)
    head_of_lane = [(lane_qk >= h * GLA_DK) & (lane_qk < (h + 1) * GLA_DK) for h in range(nh)]
    zeros_v = jnp.zeros((c, GLA_DV), BF16)
    gnorm = gn_ref[...]

    def stack_heads(a):
        return jnp.concatenate([jnp.where(head_of_lane[h], a, 0.0) for h in range(nh)],
                               axis=0).astype(BF16)

    def block_diag(blocks):
        rows = []
        for h in range(nh):
            rows.append(jnp.concatenate(
                [blocks[h] if hh == h else zeros_v for hh in range(nh)], axis=1))
        return jnp.concatenate(rows, axis=0)

    def level_sums(b, length):
        parts = []
        for blk in range(0, c, 2 * length):
            b_rho = b[blk + length - 1:blk + length]
            parts.append(b_rho - b[blk:blk + length])
            parts.append(b[blk + length:blk + 2 * length] - b_rho)
        return jnp.concatenate(parts, axis=0)

    def chunk(ci):
        r0 = pl.multiple_of(ci * c, c)
        rows = pl.ds(r0, c)
        q = q_ref[rows, :].astype(F32)
        k = k_ref[rows, :].astype(F32)
        v = v_ref[rows, :]
        g = g_ref[rows, :]
        g_hi = g.astype(BF16)
        g_lo = (g - g_hi.astype(F32)).astype(BF16)
        g2 = jnp.concatenate([g_hi, g_lo], axis=1)
        x2 = _dot(pm_ref[...], g2)
        xs = x2[:, 0:GLA_QK] + x2[:, GLA_QK:2 * GLA_QK]
        b = xs[0:c]
        yield False
        qe = (q * jnp.exp(b)).astype(BF16)
        kd = (k * jnp.exp(b[c - 1:c] - b)).astype(BF16)
        upd = _dot_tn(kd, v)
        b_last_col = jnp.transpose(jnp.broadcast_to(b[c - 1:c], (LANES, GLA_QK)))
        yield False

        att = _dot_nt(q.astype(BF16), stack_heads(k)) * mk_ref[GLA_LEVELS]
        yield False
        small = GLA_LEVELS - (GLA_PROWS // c - 1)
        for lv in range(GLA_LEVELS):
            if lv < small:
                x_lv = level_sums(b, c >> (lv + 1))
            else:
                x_lv = xs[(1 + lv - small) * c:(2 + lv - small) * c]
            e = jnp.exp(x_lv)
            att = att + _dot_nt((q * e).astype(BF16), stack_heads(k * e)) * mk_ref[lv]
            yield False

        v_heads = [v[:, h * GLA_DV:(h + 1) * GLA_DV] for h in range(nh)]
        lhs = jnp.concatenate([att.astype(BF16), qe], axis=1)
        vbd = block_diag(v_heads)
        dcol = jnp.exp(b_last_col)
        rr = r_ref[rows, :].astype(F32)
        gate = rr * jax.nn.sigmoid(rr)
        yield True

        s_heads = [st_ref[h].astype(BF16) for h in range(nh)]
        rhs = jnp.concatenate([vbd, block_diag(s_heads)], axis=0)
        o = _dot(lhs, rhs)
        for h in range(nh):
            rs = slice(h * GLA_DK, (h + 1) * GLA_DK)
            st_ref[h] = dcol[rs] * st_ref[h] + upd[rs, h * GLA_DV:(h + 1) * GLA_DV]
        for h in range(nh):
            cs = slice(h * GLA_DV, (h + 1) * GLA_DV)
            oh = o[:, cs]
            ms = jnp.mean(oh * oh, axis=-1, keepdims=True)
            o_ref[rows, cs] = (oh * lax.rsqrt(ms + RMS_EPS) * gnorm * gate[:, cs]).astype(BF16)

    def chunk_group(gi, carry):
        gens = [chunk(gi * GLA_GROUP + i) for i in range(GLA_GROUP)]
        live = list(gens)
        while live:
            for gen in list(live):
                if next(gen):
                    live.remove(gen)
        for gen in gens:
            for _ in gen:
                pass
        return carry

    lax.fori_loop(0, tg // (c * GLA_GROUP), chunk_group, 0)


def _gla_call(gq, gk, gv, gr, la, pm, mk, gn):
    tg = TG_GLA
    nt = SEQ // tg
    row = lambda w: pl.BlockSpec((tg, w), lambda b, i: (b * nt + i, 0))
    return pl.pallas_call(
        functools.partial(_gla_kernel, tg=tg),
        grid=(BATCH, nt),
        in_specs=[row(GLA_QK), row(GLA_QK), row(GLA_VW), row(GLA_VW), row(GLA_QK),
                  _const_spec((GLA_PROWS, GLA_CHUNK)),
                  _const_spec((GLA_LEVELS + 1, GLA_CHUNK, GLA_HEADS * GLA_CHUNK)),
                  _const_spec((1, GLA_DV))],
        out_specs=row(GLA_VW),
        out_shape=jax.ShapeDtypeStruct((TOKENS, GLA_VW), BF16),
        scratch_shapes=[pltpu.VMEM((GLA_HEADS, GLA_DK, GLA_DV), F32)],
        compiler_params=_params("arbitrary", "arbitrary"),
        name="gla_mixer",
    )(gq, gk, gv, gr, la, pm, mk, gn)


def _mla_kernel(qt_ref, k_ref, vt_ref, o_ref, m_ref, acc_ref, st_ref, mx_ref, p_ref, a_ref, *, tile):
    qi = pl.program_id(1)
    qw = tile // 2
    chains = [(h, qh) for h in range(MLA_HEADS) for qh in range(2)]
    n = len(chains)
    ring = MLA_RING
    chunk = MLA_ROW_CHUNK

    def lanes_of(qh):
        return slice(qh * qw, (qh + 1) * qw)

    def scores(j, c, diagonal):
        h, qh = chains[c]
        cols = slice(h * LANES, (h + 1) * LANES)
        k0 = pl.multiple_of(j * tile, tile)
        st = _dot(k_ref[pl.ds(k0, tile), cols], qt_ref[0, cols, lanes_of(qh)])
        if diagonal:
            r = lax.broadcasted_iota(jnp.int32, st.shape, 0)
            col = lax.broadcasted_iota(jnp.int32, st.shape, 1) + qh * qw
            st = jnp.where(r <= col, st, -jnp.inf)
        st_ref[c % ring] = st

    def max_pass(c):
        slot = c % ring
        mx = None
        for r0 in range(0, tile, chunk):
            blk = st_ref[slot, r0:r0 + chunk, :]
            mx = blk if mx is None else jnp.maximum(mx, blk)
            yield
        mx_ref[slot] = jnp.max(mx, axis=0, keepdims=True)

    def exp_pass(c):
        h, qh = chains[c]
        slot = c % ring
        m_prev = m_ref[h, :, lanes_of(qh)]
        m_new = jnp.maximum(m_prev, mx_ref[slot])
        m_ref[h, :, lanes_of(qh)] = m_new
        a_ref[slot] = jnp.exp2(m_prev - m_new)
        for r0 in range(0, tile, chunk):
            p_ref[slot, r0:r0 + chunk, :] = jnp.exp2(st_ref[slot, r0:r0 + chunk, :] - m_new).astype(BF16)
            yield

    def alternate(*gens):
        live = list(gens)
        while live:
            for g in list(live):
                try:
                    next(g)
                except StopIteration:
                    live.remove(g)

    def weighted_values(j, c):
        h, qh = chains[c]
        slot = c % ring
        rows = slice(h * MLA_VT_ROWS, (h + 1) * MLA_VT_ROWS)
        pv = _dot(vt_ref[j, rows, :], p_ref[slot])
        acc_ref[h, :, lanes_of(qh)] = a_ref[slot] * acc_ref[h, :, lanes_of(qh)] + pv

    def pipeline_steps(j, steps, diagonal):
        j_prev = jnp.maximum(j - 1, 0)
        for t in steps:
            if t < n:
                scores(j, t, diagonal)
            gens = []
            if t - 1 < n:
                gens.append(max_pass((t - 1) % n))
            if t - 2 < n:
                gens.append(exp_pass((t - 2) % n))
            alternate(*gens)
            if t - 3 < n:
                weighted_values(j if t >= 3 else j_prev, (t - 3) % n)

    m_ref[...] = jnp.full_like(m_ref, MLA_M_INIT)
    acc_ref[...] = jnp.zeros_like(acc_ref)
    st_ref[(n - 1) % ring] = jnp.full((tile, qw), -jnp.inf, F32)
    st_ref[(n - 2) % ring] = jnp.full((tile, qw), -jnp.inf, F32)
    mx_ref[(n - 2) % ring] = jnp.full((1, qw), -jnp.inf, F32)
    p_ref[(n - 3) % ring] = jnp.zeros((tile, qw), BF16)
    a_ref[(n - 3) % ring] = jnp.ones((1, qw), F32)

    def tiles(first, count):
        for u in range(count):
            pipeline_steps(first + u, range(n), False)

    def body(i, carry):
        tiles(MLA_UNROLL * i, MLA_UNROLL)
        return carry

    lax.fori_loop(0, qi // MLA_UNROLL, body, 0)

    rem = qi % MLA_UNROLL
    run = MLA_UNROLL // 2
    while run >= 1:
        @pl.when((rem // run) % 2 == 1)
        def _():
            tiles(qi - rem % (2 * run), run)
        run //= 2

    pipeline_steps(qi, range(n), True)
    pipeline_steps(qi, range(n, n + 3), True)

    for h in range(MLA_HEADS):
        num = acc_ref[h, 0:MLA_V, :]
        den = acc_ref[h, MLA_V:MLA_V + 1, :]
        o_ref[:, h * LANES:(h + 1) * LANES] = (num / den).T.astype(BF16)


def _mla_call(qt, k, vt):
    tile = TQ_MLA
    assert tile == TK_MLA == TM_ODD
    nt = SEQ // tile
    qw = tile // 2
    vt_rows = MLA_HEADS * MLA_VT_ROWS
    once = pl.Buffered(1)
    return pl.pallas_call(
        functools.partial(_mla_kernel, tile=tile),
        grid=(BATCH, nt),
        in_specs=[pl.BlockSpec((1, MLA_W, tile), lambda b, i: (b * nt + i, 0, 0)),
                  pl.BlockSpec((SEQ, MLA_W), lambda b, i: (b, 0), pipeline_mode=once),
                  pl.BlockSpec((nt, vt_rows, tile), lambda b, i: (b, 0, 0), pipeline_mode=once)],
        out_specs=pl.BlockSpec((tile, MLA_W), lambda b, i: (b * nt + i, 0)),
        out_shape=jax.ShapeDtypeStruct((TOKENS, MLA_W), BF16),
        scratch_shapes=[pltpu.VMEM((MLA_HEADS, 1, tile), F32),
                        pltpu.VMEM((MLA_HEADS, MLA_VT_ROWS, tile), F32),
                        pltpu.VMEM((MLA_RING, tile, qw), F32),
                        pltpu.VMEM((MLA_RING, 1, qw), F32),
                        pltpu.VMEM((MLA_RING, tile, qw), BF16),
                        pltpu.VMEM((MLA_RING, 1, qw), F32)],
        compiler_params=_params("parallel", "arbitrary"),
        name="mla_attn",
    )(qt, k, vt)


def _rope_tables(dtype):
    pos = jnp.arange(SEQ, dtype=F32)
    inv_freq = ROPE_THETA ** (-jnp.arange(0, MLA_ROPE, 2, dtype=F32) / MLA_ROPE)
    ang = pos[:, None] * inv_freq[None, :]
    cos, sin = jnp.cos(ang).astype(dtype), jnp.sin(ang).astype(dtype)
    zeros = jnp.zeros((SEQ, LANES - MLA_NOPE - MLA_ROPE), dtype)
    ctab = jnp.concatenate([jnp.ones((SEQ, MLA_NOPE), dtype), cos, cos, zeros], axis=1)
    stab = jnp.concatenate([jnp.zeros((SEQ, MLA_NOPE), dtype), -sin, sin, zeros], axis=1)
    return ctab, stab


def _swap_halves(w):
    half = w.shape[-1] // 2
    return jnp.concatenate([w[..., half:], w[..., :half]], axis=-1)


def _place_rope(w):
    rows = w.shape[0]
    return jnp.concatenate([jnp.zeros((rows, MLA_NOPE), w.dtype), w,
                            jnp.zeros((rows, LANES - MLA_NOPE - MLA_ROPE), w.dtype)], axis=1)


def _odd_weights(w_in, w_gate, w_uq, w_ukv):
    o = 0
    gq = w_in[:, o:o + GLA_QK]; o += GLA_QK
    gk = w_in[:, o:o + GLA_QK]; o += GLA_QK
    gv = w_in[:, o:o + GLA_VW]; o += GLA_VW
    g_low = w_in[:, o:o + GLA_GATE_RANK]; o += GLA_GATE_RANK
    gr = w_in[:, o:o + GLA_VW]; o += GLA_VW
    c_q = w_in[:, o:o + MLA_Q_RANK]; o += MLA_Q_RANK
    c_kv = w_in[:, o:o + MLA_KV_RANK]; o += MLA_KV_RANK
    k_r = w_in[:, o:o + MLA_ROPE]
    pad = jnp.zeros((D_MODEL, LANES - GLA_GATE_RANK), w_in.dtype)
    w_ext = jnp.concatenate([gq, gk, gv, gr, c_q, c_kv, g_low, pad,
                             _place_rope(k_r), _place_rope(_swap_halves(k_r))], axis=1)
    wgate = jnp.concatenate([w_gate, jnp.zeros((LANES - GLA_GATE_RANK, GLA_QK), w_gate.dtype)], axis=0)

    per_q = MLA_NOPE + MLA_ROPE
    qa, qb = [], []
    zq = jnp.zeros((MLA_Q_RANK, LANES - per_q), w_uq.dtype)
    for h in range(MLA_HEADS):
        nope = w_uq[:, h * per_q:h * per_q + MLA_NOPE]
        rope = w_uq[:, h * per_q + MLA_NOPE:(h + 1) * per_q]
        qa.append(jnp.concatenate([nope, rope, zq], axis=1))
        qb.append(_place_rope(_swap_halves(rope)))
    wq = jnp.concatenate(qa + qb, axis=1)

    per_kv = MLA_NOPE + MLA_V
    ks, vs = [], []
    zk = jnp.zeros((MLA_KV_RANK, LANES - MLA_NOPE), w_ukv.dtype)
    for h in range(MLA_HEADS):
        ks.append(jnp.concatenate([w_ukv[:, h * per_kv:h * per_kv + MLA_NOPE], zk], axis=1))
        vs.append(w_ukv[:, h * per_kv + MLA_NOPE:(h + 1) * per_kv])
    wkv = jnp.concatenate(ks + vs, axis=1)
    return w_ext.astype(BF16), wgate.astype(BF16), wq.astype(BF16), wkv.astype(BF16)


def kernel(x, ev_w_in, ev_conv_w, ev_sinks, ev_w_out, od_w_in, od_gla_w_gate, od_gla_b_gate,
           od_gla_norm_g, od_mla_q_norm_g, od_mla_w_uq, od_mla_kv_norm_g, od_mla_w_ukv, od_w_out,
           ffn_w_gate, ffn_w_up, ffn_w_down, ln_mix_g, ln_mix_b, ln_ffn_g, ln_ffn_b):
    assert x.shape == (BATCH, SEQ, D_MODEL) and x.dtype == F32
    ctab, stab = _rope_tables(x.dtype)
    pm = jnp.asarray(_gla_prefix_matrix(), BF16)
    mk = jnp.asarray(_gla_pair_masks(), F32)
    half = D_MODEL // 2
    h = x.reshape(TOKENS, D_MODEL)
    for layer in range(DEPTH):
        i = layer // 2
        w_out = (ev_w_out if layer % 2 == 0 else od_w_out)[i]
        post_args = (w_out[:half].astype(BF16), w_out[half:].astype(BF16),
                     ln_mix_g[layer][None, :], ln_mix_b[layer][None, :],
                     ffn_w_gate[layer].astype(BF16), ffn_w_up[layer].astype(BF16),
                     ffn_w_down[layer].astype(BF16),
                     ln_ffn_g[layer][None, :], ln_ffn_b[layer][None, :])
        if layer % 2 == 0:
            y1, y2 = _even_call(h, ev_w_in[i].astype(BF16), ev_conv_w[i], ev_sinks[i])
        else:
            w_ext, wgate, wq, wkv = _odd_weights(od_w_in[i], od_gla_w_gate[i],
                                                 od_mla_w_uq[i], od_mla_w_ukv[i])
            gq, gk, gv, gr, la, mq, mk_, mv = _odd_proj_call(
                h, w_ext, wgate, od_gla_b_gate[i][None, :], od_mla_q_norm_g[i][None, :], wq,
                od_mla_kv_norm_g[i][None, :], wkv, ctab, stab)
            y1 = _gla_call(gq, gk, gv, gr, la, pm, mk, od_gla_norm_g[i][None, :])
            y2 = _mla_call(mq, mk_, mv)
        h = _post_call(h, y1, y2, *post_args)
    return h.reshape(BATCH, SEQ, D_MODEL)
```
